```python
import math
import jax, jax.numpy as jnp
from jax import lax
import numpy as np

D_MODEL = 1024
BATCH = 32
SEQ = 2048
DEPTH = 2

PLE_DIM = 256
MIX_WIDTH = D_MODEL
N_GROUPS_MIX = 4
GROUP_WIDTH = MIX_WIDTH // N_GROUPS_MIX
BLOCK = 128

MLA_HEADS = 4
MLA_Q_RANK = 256
MLA_KV_RANK = 128
MLA_NOPE = 64
MLA_ROPE = 32
MLA_V = GROUP_WIDTH // MLA_HEADS
ROPE_THETA = 10000.0

HY_WIDTH = GROUP_WIDTH
HY_ORDER = 2
HY_EMB = 33
HY_BANDS = (HY_EMB - 1) // 2
HY_FILTER_HIDDEN = 64
HY_FAST_DECAY = 0.3
HY_SLOW_DECAY = 1.5
HY_TARGET = 1e-2

SWA_HEADS = 4
SWA_KV_HEADS = 2
SWA_HEAD_DIM = GROUP_WIDTH // SWA_HEADS
SWA_WINDOW = 128

SSD_D_INNER = GROUP_WIDTH
SSD_HEAD_DIM = 64
SSD_HEADS = SSD_D_INNER // SSD_HEAD_DIM
SSD_GROUPS = 2
SSD_STATE = 128
SSD_CHUNK = BLOCK

D_FF = 2816
SHORT_CONV = 3

LN_EPS = 1e-5
RMS_EPS = 1e-6
NEG_INF = -1e30
ALPHA = (2.0 * DEPTH) ** 0.25
BETA = (8.0 * DEPTH) ** -0.25

IN_SPLITS = [MLA_Q_RANK, MLA_KV_RANK, MLA_ROPE,
             (HY_ORDER + 1) * HY_WIDTH,
             SWA_HEADS * SWA_HEAD_DIM, SWA_KV_HEADS * SWA_HEAD_DIM, SWA_KV_HEADS * SWA_HEAD_DIM,
             SSD_D_INNER, SSD_D_INNER + 2 * SSD_GROUPS * SSD_STATE, 2 * SSD_HEADS]
IN_WIDTH = sum(IN_SPLITS)
IN_OFFSETS = [int(v) for v in np.cumsum(IN_SPLITS)[:-1]]

kernel_name = "hybrid_parallel_group_encoder"


def layer_norm(x, g, b):
    xf = x.astype(jnp.float32)
    mu = jnp.mean(xf, -1, keepdims=True)
    var = jnp.mean(jnp.square(xf - mu), -1, keepdims=True)
    return ((xf - mu) * lax.rsqrt(var + LN_EPS) * g + b).astype(x.dtype)


def rms_norm(x, g):
    xf = x.astype(jnp.float32)
    return (xf * lax.rsqrt(jnp.mean(xf * xf, -1, keepdims=True) + RMS_EPS) * g).astype(x.dtype)


def group_rms_norm(y, g, n_groups):
    bsz, s, w = y.shape
    yf = y.astype(jnp.float32).reshape(bsz, s, n_groups, w // n_groups)
    yf = yf * lax.rsqrt(jnp.mean(yf * yf, -1, keepdims=True) + RMS_EPS)
    return (yf.reshape(bsz, s, w) * g).astype(y.dtype)


def dwconv_centred(x, w, b):
    k = w.shape[0]
    s = x.shape[1]
    half = (k - 1) // 2
    xp = jnp.pad(x, ((0, 0), (half, half), (0, 0)))
    out = xp[:, 0:s] * w[0]
    for j in range(1, k):
        out = out + xp[:, j:j + s] * w[j]
    return out + b


def rotary_tables(seq_len):
    inv_freq = ROPE_THETA ** (-jnp.arange(0, MLA_ROPE, 2, dtype=jnp.float32) / MLA_ROPE)
    ang = jnp.arange(seq_len, dtype=jnp.float32)[:, None] * inv_freq[None, :]
    return jnp.cos(ang), jnp.sin(ang)


def apply_rope(x, cos, sin):
    extra = x.ndim - 3
    c = cos.reshape(cos.shape[:1] + (1,) * extra + cos.shape[1:])
    sn = sin.reshape(sin.shape[:1] + (1,) * extra + sin.shape[1:])
    x1, x2 = jnp.split(x.astype(jnp.float32), 2, axis=-1)
    return jnp.concatenate([x1 * c - x2 * sn, x2 * c + x1 * sn], -1).astype(x.dtype)


def alibi_slopes(n):
    start = 2.0 ** (-8.0 / n)
    return start ** jnp.arange(1, n + 1, dtype=jnp.float32)


def mla_mixer(cq, ckv, kr, gq, gkv, w_uq, w_ukv, cos, sin):
    bsz, s, _ = cq.shape
    q = (rms_norm(cq, gq) @ w_uq).reshape(bsz, s, MLA_HEADS, MLA_NOPE + MLA_ROPE)
    q_nope = q[..., :MLA_NOPE]
    q_rope = apply_rope(q[..., MLA_NOPE:], cos, sin)
    kv = (rms_norm(ckv, gkv) @ w_ukv).reshape(bsz, s, MLA_HEADS, MLA_NOPE + MLA_V)
    k_nope = kv[..., :MLA_NOPE]
    v = kv[..., MLA_NOPE:]
    k_rope = apply_rope(kr, cos, sin)
    scale = (MLA_NOPE + MLA_ROPE) ** -0.5
    nb = s // BLOCK
    qn_b = q_nope.reshape(bsz, nb, BLOCK, MLA_HEADS, MLA_NOPE).transpose(1, 0, 2, 3, 4)
    qr_b = q_rope.reshape(bsz, nb, BLOCK, MLA_HEADS, MLA_ROPE).transpose(1, 0, 2, 3, 4)

    def attend(blk):
        qn, qr = blk
        sc = (jnp.einsum('bqhd,bkhd->bhqk', qn, k_nope, preferred_element_type=jnp.float32)
              + jnp.einsum('bqhr,bkr->bhqk', qr, k_rope, preferred_element_type=jnp.float32)) * scale
        pr = jax.nn.softmax(sc, axis=-1).astype(v.dtype)
        return jnp.einsum('bhqk,bkhd->bqhd', pr, v)

    o = lax.map(attend, (qn_b, qr_b))
    return o.transpose(1, 0, 2, 3, 4).reshape(bsz, s, MLA_HEADS * MLA_V)


def hyena_filters(seq_len, w1, b1, freq, w2, b2, w3):
    f32 = jnp.float32
    t = jnp.linspace(0.0, 1.0, seq_len, dtype=f32)[:, None]
    ang = 2.0 * math.pi * jnp.arange(seq_len, dtype=f32)[:, None] / seq_len
    bands = jnp.linspace(1e-4, HY_BANDS - 1, HY_BANDS, dtype=f32)[None, :]
    feat = jnp.concatenate([t, jnp.cos(bands * ang), -jnp.sin(bands * ang)], -1)
    fr = freq.astype(f32)
    h = jnp.sin(fr * (feat @ w1.astype(f32) + b1.astype(f32)))
    h = jnp.sin(fr * (h @ w2.astype(f32) + b2.astype(f32)))
    h = (h @ w3.astype(f32)).reshape(seq_len, HY_ORDER, 2, HY_WIDTH)
    max_decay = math.log(HY_TARGET) / HY_FAST_DECAY
    min_decay = math.log(HY_TARGET) / HY_SLOW_DECAY
    deltas = jnp.linspace(min_decay, max_decay, HY_WIDTH, dtype=f32)
    decay = jnp.exp(-t * jnp.abs(deltas)[None, :])
    return h * decay[:, None, None, :]


def bidir_fft_conv(z, h_fwd, h_bwd):
    s = z.shape[1]
    k = jnp.concatenate([h_fwd, jnp.zeros((1, h_fwd.shape[1]), h_fwd.dtype), h_bwd[:0:-1]], 0)
    kf = jnp.fft.rfft(k, axis=0)
    zf = jnp.fft.rfft(z, n=2 * s, axis=1)
    return jnp.fft.irfft(zf * kf[None], n=2 * s, axis=1)[:, :s]


def hyena_mixer(u, conv_w, conv_b, w1, b1, freq, w2, b2, w3, filt_bias):
    s = u.shape[1]
    uc = dwconv_centred(u, conv_w, conv_b).astype(jnp.float32)
    v, x1, x2 = jnp.split(uc, 3, axis=-1)
    h = hyena_filters(s, w1, b1, freq, w2, b2, w3)
    z = v
    for n, gate in enumerate((x1, x2)):
        z = gate * (bidir_fft_conv(z, h[:, n, 0], h[:, n, 1]) + filt_bias[n].astype(jnp.float32) * z)
    return z.astype(u.dtype)


def swa_mixer(q, k, v, sink):
    bsz, s, _ = q.shape
    nb = s // BLOCK
    grp = SWA_HEADS // SWA_KV_HEADS
    hd = SWA_HEAD_DIM
    q = q.reshape(bsz, nb, BLOCK, SWA_KV_HEADS, grp, hd)
    pad = ((0, 0), (BLOCK, BLOCK), (0, 0), (0, 0))
    kp = jnp.pad(k.reshape(bsz, s, SWA_KV_HEADS, hd), pad).reshape(bsz, nb + 2, BLOCK, SWA_KV_HEADS, hd)
    vp = jnp.pad(v.reshape(bsz, s, SWA_KV_HEADS, hd), pad).reshape(bsz, nb + 2, BLOCK, SWA_KV_HEADS, hd)
    kw = jnp.concatenate([kp[:, :nb], kp[:, 1:nb + 1], kp[:, 2:]], axis=2)
    vw = jnp.concatenate([vp[:, :nb], vp[:, 1:nb + 1], vp[:, 2:]], axis=2)
    sc = jnp.einsum('bnqkgd,bnjkd->bnkgqj', q, kw, preferred_element_type=jnp.float32) * (hd ** -0.5)
    blk = jnp.arange(nb)[:, None] * BLOCK
    qpos = blk + jnp.arange(BLOCK)[None, :]
    kpos = blk - BLOCK + jnp.arange(3 * BLOCK)[None, :]
    dist = jnp.abs(qpos[:, :, None] - kpos[:, None, :])
    valid = (dist <= SWA_WINDOW) & ((kpos >= 0) & (kpos < s))[:, None, :]
    slopes = alibi_slopes(SWA_HEADS).reshape(SWA_KV_HEADS, grp)
    bias = -slopes[None, :, :, None, None] * dist.astype(jnp.float32)[:, None, None]
    sc = jnp.where(valid[:, None, None], sc + bias, NEG_INF)
    sk = sink.astype(jnp.float32).reshape(SWA_KV_HEADS, grp)[:, :, None]
    m = jnp.maximum(jnp.max(sc, -1), sk)
    e = jnp.exp(sc - m[..., None])
    denom = jnp.sum(e, -1) + jnp.exp(sk - m)
    pr = (e / denom[..., None]).astype(v.dtype)
    o = jnp.einsum('bnkgqj,bnjkd->bnqkgd', pr, vw)
    return o.reshape(bsz, s, SWA_HEADS * hd)


def ssd_chunked(x, a, bm, cm):
    b, s, h, p = x.shape
    c = s // SSD_CHUNK
    qn = SSD_CHUNK
    g = SSD_GROUPS
    r = h // g
    n = SSD_STATE
    X = x.reshape(b, c, qn, g, r, p)
    A = a.reshape(b, c, qn, g, r).transpose(0, 3, 4, 1, 2)
    Bc = bm.reshape(b, c, qn, g, n)
    Cc = cm.reshape(b, c, qn, g, n)
    a_cs = jnp.cumsum(A, axis=-1)
    seg = a_cs[..., :, None] - a_cs[..., None, :]
    tril = jnp.tril(jnp.ones((qn, qn), dtype=bool))
    lmat = jnp.exp(jnp.where(tril, seg, -jnp.inf))
    cb = jnp.einsum('bctgn,bcsgn->bgcts', Cc, Bc)
    y_diag = jnp.einsum('bgcts,bgrcts,bcsgrp->bctgrp', cb, lmat, X)
    decay_states = jnp.exp(a_cs[..., -1:] - a_cs)
    states = jnp.einsum('bcsgn,bgrcs,bcsgrp->cbgrpn', Bc, decay_states, X)
    chunk_decay = jnp.exp(a_cs[..., -1]).transpose(3, 0, 1, 2)

    def step(carry, inp):
        st, dec = inp
        return carry * dec[..., None, None] + st, carry

    _, prev = lax.scan(step, jnp.zeros(states.shape[1:], jnp.float32), (states, chunk_decay))
    y_off = jnp.einsum('bctgn,cbgrpn,bgrct->bctgrp', Cc, prev, jnp.exp(a_cs))
    return (y_diag + y_off).reshape(b, s, h, p)


def ssd_mixer(xbc, dt_raw, conv_w, conv_b, dt_bias, a_log, d_skip):
    bsz, s, _ = xbc.shape
    f32 = jnp.float32
    xbc = jax.nn.silu(dwconv_centred(xbc, conv_w, conv_b).astype(f32))
    xs, bm, cm = jnp.split(xbc, [SSD_D_INNER, SSD_D_INNER + SSD_GROUPS * SSD_STATE], axis=-1)
    xs = xs.reshape(bsz, s, SSD_HEADS, SSD_HEAD_DIM)
    bm = bm.reshape(bsz, s, SSD_GROUPS, SSD_STATE)
    cm = cm.reshape(bsz, s, SSD_GROUPS, SSD_STATE)
    dt = jax.nn.softplus(dt_raw.astype(f32).reshape(bsz, s, 2, SSD_HEADS) + dt_bias.astype(f32))
    a = -jnp.exp(a_log.astype(f32))
    y_f = ssd_chunked(xs * dt[:, :, 0, :, None], dt[:, :, 0] * a[0], bm, cm)
    flip = lambda t: jnp.flip(t, axis=1)
    y_b = flip(ssd_chunked(flip(xs * dt[:, :, 1, :, None]), flip(dt[:, :, 1] * a[1]), flip(bm), flip(cm)))
    dsum = (d_skip[0] + d_skip[1]).astype(f32)[:, None]
    y = y_f + y_b + dsum * xs
    return y.reshape(bsz, s, SSD_D_INNER)


def setup_inputs(seed: int = 0) -> dict:
    key = jax.random.key(seed)
    ks = iter(jax.random.split(key, 64))
    f32 = jnp.float32
    L = DEPTH
    D = D_MODEL

    def nrm(shape, scale):
        return jax.random.normal(next(ks), shape, f32) * scale

    def gain(shape):
        return 1.0 + nrm(shape, 0.02)

    x = nrm((BATCH, SEQ, D), 1.0)
    p = nrm((DEPTH, BATCH, SEQ, PLE_DIM), 1.0)
    emb_ln_g = gain((D,))
    emb_ln_b = nrm((D,), 0.02)
    col_scale = np.ones(IN_WIDTH, np.float32)
    sv0 = IN_OFFSETS[5]
    col_scale[sv0:sv0 + SWA_KV_HEADS * SWA_HEAD_DIM] = BETA
    w_in = nrm((L, D, IN_WIDTH), D ** -0.5) * jnp.asarray(col_scale)
    mla_q_norm = gain((L, MLA_Q_RANK))
    mla_kv_norm = gain((L, MLA_KV_RANK))
    mla_w_uq = nrm((L, MLA_Q_RANK, MLA_HEADS * (MLA_NOPE + MLA_ROPE)), MLA_Q_RANK ** -0.5)
    ukv_scale = jnp.concatenate([jnp.ones((MLA_NOPE,), f32), jnp.full((MLA_V,), BETA, f32)])
    mla_w_ukv = (nrm((L, MLA_KV_RANK, MLA_HEADS, MLA_NOPE + MLA_V), MLA_KV_RANK ** -0.5)
                 * ukv_scale).reshape(L, MLA_KV_RANK, MLA_HEADS * (MLA_NOPE + MLA_V))
    hy_conv_w = nrm((L, SHORT_CONV, (HY_ORDER + 1) * HY_WIDTH), SHORT_CONV ** -0.5)
    hy_conv_b = nrm((L, (HY_ORDER + 1) * HY_WIDTH), 0.02)
    hy_f_w1 = nrm((L, HY_EMB, HY_FILTER_HIDDEN), HY_EMB ** -0.5)
    hy_f_b1 = nrm((L, HY_FILTER_HIDDEN), 0.1)
    hy_f_freq = 1.0 + nrm((L, HY_FILTER_HIDDEN), 0.1)
    hy_f_w2 = nrm((L, HY_FILTER_HIDDEN, HY_FILTER_HIDDEN), HY_FILTER_HIDDEN ** -0.5)
    hy_f_b2 = nrm((L, HY_FILTER_HIDDEN), 0.1)
    hy_f_w3 = nrm((L, HY_FILTER_HIDDEN, HY_ORDER * 2 * HY_WIDTH), 0.1 * HY_FILTER_HIDDEN ** -0.5)
    hy_bias = nrm((L, HY_ORDER, HY_WIDTH), 0.5)
    swa_sink = nrm((L, SWA_HEADS), 0.5)
    ssd_conv_w = nrm((L, SHORT_CONV, SSD_D_INNER + 2 * SSD_GROUPS * SSD_STATE), SHORT_CONV ** -0.5)
    ssd_conv_b = nrm((L, SSD_D_INNER + 2 * SSD_GROUPS * SSD_STATE), 0.02)
    dt0 = jnp.exp(jax.random.uniform(next(ks), (L, 2, SSD_HEADS), f32, math.log(1e-3), math.log(1e-1)))
    ssd_dt_bias = dt0 + jnp.log(-jnp.expm1(-dt0))
    ssd_a_log = jnp.log(jax.random.uniform(next(ks), (L, 2, SSD_HEADS), f32, 1.0, 16.0))
    ssd_d = 1.0 + nrm((L, 2, SSD_HEADS), 0.1)
    mix_norm_g = gain((L, MIX_WIDTH))
    w_out = nrm((L, MIX_WIDTH, D), BETA * MIX_WIDTH ** -0.5)
    ln1_g = gain((L, D))
    ln1_b = nrm((L, D), 0.02)
    ffn_w_gate = nrm((L, D, D_FF), BETA * D ** -0.5)
    ffn_w_up = nrm((L, D, D_FF), BETA * D ** -0.5)
    ffn_conv_w = nrm((L, SHORT_CONV, D_FF), SHORT_CONV ** -0.5)
    ffn_conv_b = nrm((L, D_FF), 0.02)
    ffn_w_down = nrm((L, D_FF, D), BETA * D_FF ** -0.5)
    ln2_g = gain((L, D))
    ln2_b = nrm((L, D), 0.02)
    ple_w_proj = nrm((L, PLE_DIM, D), BETA * PLE_DIM ** -0.5)
    ple_w_gate = nrm((L, D, D), D ** -0.5)
    ple_b_gate = nrm((L, D), 0.02)
    ln3_g = gain((L, D))
    ln3_b = nrm((L, D), 0.02)
    return {"x": x, "p": p, "emb_ln_g": emb_ln_g, "emb_ln_b": emb_ln_b, "w_in": w_in,
            "mla_q_norm": mla_q_norm, "mla_kv_norm": mla_kv_norm, "mla_w_uq": mla_w_uq, "mla_w_ukv": mla_w_ukv,
            "hy_conv_w": hy_conv_w, "hy_conv_b": hy_conv_b, "hy_f_w1": hy_f_w1, "hy_f_b1": hy_f_b1,
            "hy_f_freq": hy_f_freq, "hy_f_w2": hy_f_w2, "hy_f_b2": hy_f_b2, "hy_f_w3": hy_f_w3, "hy_bias": hy_bias,
            "swa_sink": swa_sink, "ssd_conv_w": ssd_conv_w, "ssd_conv_b": ssd_conv_b, "ssd_dt_bias": ssd_dt_bias,
            "ssd_a_log": ssd_a_log, "ssd_d": ssd_d, "mix_norm_g": mix_norm_g, "w_out": w_out,
            "ln1_g": ln1_g, "ln1_b": ln1_b, "ffn_w_gate": ffn_w_gate, "ffn_w_up": ffn_w_up,
            "ffn_conv_w": ffn_conv_w, "ffn_conv_b": ffn_conv_b, "ffn_w_down": ffn_w_down,
            "ln2_g": ln2_g, "ln2_b": ln2_b, "ple_w_proj": ple_w_proj, "ple_w_gate": ple_w_gate,
            "ple_b_gate": ple_b_gate, "ln3_g": ln3_g, "ln3_b": ln3_b}


def reference(x, p, emb_ln_g, emb_ln_b, w_in, mla_q_norm, mla_kv_norm, mla_w_uq, mla_w_ukv,
              hy_conv_w, hy_conv_b, hy_f_w1, hy_f_b1, hy_f_freq, hy_f_w2, hy_f_b2, hy_f_w3, hy_bias,
              swa_sink, ssd_conv_w, ssd_conv_b, ssd_dt_bias, ssd_a_log, ssd_d, mix_norm_g, w_out,
              ln1_g, ln1_b, ffn_w_gate, ffn_w_up, ffn_conv_w, ffn_conv_b, ffn_w_down,
              ln2_g, ln2_b, ple_w_proj, ple_w_gate, ple_b_gate, ln3_g, ln3_b):
    s = x.shape[1]
    cos, sin = rotary_tables(s)
    h = layer_norm(x, emb_ln_g, emb_ln_b)
    for i in range(DEPTH):
        u = h @ w_in[i]
        (cq, ckv, kr, hy_u, sq, sk, sv, ssd_z, ssd_xbc, ssd_dt) = jnp.split(u, IN_OFFSETS, axis=-1)
        y_a = mla_mixer(cq, ckv, kr, mla_q_norm[i], mla_kv_norm[i], mla_w_uq[i], mla_w_ukv[i], cos, sin)
        y_b = hyena_mixer(hy_u, hy_conv_w[i], hy_conv_b[i], hy_f_w1[i], hy_f_b1[i], hy_f_freq[i],
                          hy_f_w2[i], hy_f_b2[i], hy_f_w3[i], hy_bias[i])
        y_c = swa_mixer(sq, sk, sv, swa_sink[i])
        y_d = ssd_mixer(ssd_xbc, ssd_dt, ssd_conv_w[i], ssd_conv_b[i], ssd_dt_bias[i], ssd_a_log[i],
                        ssd_d[i]).astype(h.dtype) * jax.nn.silu(ssd_z)
        y = jnp.concatenate([y_a, y_b.astype(h.dtype), y_c, y_d], axis=-1)
        y = group_rms_norm(y, mix_norm_g[i], N_GROUPS_MIX)
        h = layer_norm(ALPHA * h + y @ w_out[i], ln1_g[i], ln1_b[i])
        gate = dwconv_centred(h @ ffn_w_gate[i], ffn_conv_w[i], ffn_conv_b[i])
        f = (jax.nn.silu(gate) * (h @ ffn_w_up[i])) @ ffn_w_down[i]
        h = layer_norm(ALPHA * h + f, ln2_g[i], ln2_b[i])
        e = (p[i] @ ple_w_proj[i]) * jax.nn.sigmoid(h @ ple_w_gate[i] + ple_b_gate[i])
        h = layer_norm(ALPHA * h + e, ln3_g[i], ln3_b[i])
    return h
```

```python
import functools
import math

import jax
import jax.numpy as jnp
import numpy as np
from jax import lax
from jax.experimental import pallas as pl
from jax.experimental.pallas import tpu as pltpu

F32 = jnp.float32
BF16 = jnp.bfloat16

D_MODEL = 1024
DEPTH = 2
PLE_DIM = 256
GROUP_WIDTH = 256
BLOCK = 128

MLA_HEADS = 4
MLA_Q_RANK = 256
MLA_KV_RANK = 128
MLA_NOPE = 64
MLA_ROPE = 32
MLA_V = 64
ROPE_THETA = 10000.0

HY_WIDTH = 256
HY_ORDER = 2
HY_EMB = 33
HY_BANDS = 16
HY_FAST_DECAY = 0.3
HY_SLOW_DECAY = 1.5
HY_TARGET = 1e-2
HY_BLK = 256

SWA_HEADS = 4
SWA_KV_HEADS = 2
SWA_HEAD_DIM = 64
SWA_WINDOW = 128

SSD_D_INNER = 256
SSD_HEAD_DIM = 64
SSD_HEADS = 4
SSD_GROUPS = 2
SSD_STATE = 128
SSD_CHUNK = 128

D_FF = 2816
LN_EPS = 1e-5
RMS_EPS = 1e-6
NEG_INF = -1e30
ALPHA = (2.0 * DEPTH) ** 0.25

LANE = 128
VMEM_LIMIT = 56 * 1024 * 1024

MLA_IN_W = 640
HY_IN_W = 768
SWA_IN_W = 768
SSD_IN_W = 1152
IN_W = MLA_IN_W + HY_IN_W + SWA_IN_W + SSD_IN_W

ROW_TILE = 512
FFN_HALO = 16
FFN_CHUNK = 1408


def _cparams(sem):
    return pltpu.CompilerParams(dimension_semantics=sem, vmem_limit_bytes=VMEM_LIMIT)


def _resident(shape):
    nd = len(shape)
    return pl.BlockSpec(shape, lambda *_: (0,) * nd, pipeline_mode=pl.Buffered(1))


def _layer_norm(x, g, b):
    mu = jnp.mean(x, -1, keepdims=True)
    xc = x - mu
    var = jnp.mean(xc * xc, -1, keepdims=True)
    return xc * lax.rsqrt(var + LN_EPS) * g + b


def _rms(x, g):
    return x * lax.rsqrt(jnp.mean(x * x, -1, keepdims=True) + RMS_EPS) * g


def _silu(x):
    return x / (1.0 + jnp.exp(-x))


def _dot(a, b):
    return jnp.dot(a, b, preferred_element_type=F32)


def _dot_nt(a, b):
    return lax.dot_general(a, b, (((1,), (1,)), ((), ())), preferred_element_type=F32)


def _dot_hi(a, b):
    return jnp.dot(a, b, preferred_element_type=F32, precision=lax.Precision.HIGHEST)


def _in_proj_kernel(apply_ln, *refs):
    if apply_ln:
        x_ref, g_ref, b_ref, w_ref, h_ref, mla_ref, hy_ref, swa_ref, ssd_ref = refs
        h = _layer_norm(x_ref[...], g_ref[...], b_ref[...])
        h_ref[...] = h
    else:
        x_ref, w_ref, mla_ref, hy_ref, swa_ref, ssd_ref = refs
        h = x_ref[...]
    u = _dot(h.astype(BF16), w_ref[...])
    o = 0
    for ref, w in ((mla_ref, MLA_IN_W), (hy_ref, HY_IN_W), (swa_ref, SWA_IN_W), (ssd_ref, SSD_IN_W)):
        ref[...] = u[:, o:o + w]
        o += w


def _in_proj(x, w, ln=None):
    n = x.shape[0]
    row = lambda wd: pl.BlockSpec((ROW_TILE, wd), lambda i: (i, 0))
    outs = [jax.ShapeDtypeStruct((n, wd), F32) for wd in (MLA_IN_W, HY_IN_W, SWA_IN_W, SSD_IN_W)]
    out_specs = [row(wd) for wd in (MLA_IN_W, HY_IN_W, SWA_IN_W, SSD_IN_W)]
    if ln is not None:
        args = (x, ln[0], ln[1], w)
        in_specs = [row(D_MODEL), _resident((1, D_MODEL)), _resident((1, D_MODEL)), _resident(w.shape)]
        outs = [jax.ShapeDtypeStruct((n, D_MODEL), F32)] + outs
        out_specs = [row(D_MODEL)] + out_specs
    else:
        args = (x, w)
        in_specs = [row(D_MODEL), _resident(w.shape)]
    return pl.pallas_call(
        functools.partial(_in_proj_kernel, ln is not None),
        out_shape=outs, grid=(n // ROW_TILE,), in_specs=in_specs, out_specs=out_specs,
        compiler_params=_cparams(("parallel",)), name="in_proj")(*args)


MLA_TQ = 256


def _mla_kernel(x_ref, ct_ref, st_ref, gq_ref, gkv_ref, wq_ref, wqs_ref, wkv_ref, o_ref,
                q_scr, k_scr, v_scr):
    seq = x_ref.shape[0]
    scale = (MLA_NOPE + MLA_ROPE) ** -0.5
    ct = ct_ref[...]
    st = st_ref[...]
    cqn = _rms(x_ref[:, 0:256], gq_ref[...]).astype(BF16)
    ckvn = _rms(x_ref[:, 256:384], gkv_ref[...]).astype(BF16)
    k_rope = x_ref[:, 384:512] * ct + x_ref[:, 512:640] * st
    for h in range(MLA_HEADS):
        sl = slice(h * LANE, (h + 1) * LANE)
        q = _dot(cqn, wq_ref[:, sl]) * ct + _dot(cqn, wqs_ref[:, sl]) * st
        q_scr[:, sl] = (q * scale).astype(BF16)
        k_scr[:, sl] = (_dot(ckvn, wkv_ref[:, sl]) + k_rope).astype(BF16)
        vsl = slice(MLA_HEADS * LANE + h * LANE, MLA_HEADS * LANE + (h + 1) * LANE)
        v_scr[:, sl] = _dot(ckvn, wkv_ref[:, vsl]).astype(BF16)

    def q_tile(t, carry):
        rows = pl.ds(pl.multiple_of(t * MLA_TQ, MLA_TQ), MLA_TQ)
        for pair in range(MLA_HEADS // 2):
            acc = None
            for h in (2 * pair, 2 * pair + 1):
                sl = slice(h * LANE, (h + 1) * LANE)
                s = _dot_nt(q_scr[rows, sl], k_scr[:, sl])
                m = jnp.max(s, -1, keepdims=True)
                e = jnp.exp(s - m)
                l = jnp.sum(e, -1, keepdims=True)
                o = _dot(e.astype(BF16), v_scr[:, sl]) / l
                acc = o if acc is None else acc + o
            o_ref[rows, pair * LANE:(pair + 1) * LANE] = acc
        return carry

    lax.fori_loop(0, seq // MLA_TQ, q_tile, 0)


def _mla(mla_in, ct, st, gq, gkv, wq, wqs, wkv):
    bsz, seq, _ = mla_in.shape
    return pl.pallas_call(
        _mla_kernel,
        out_shape=jax.ShapeDtypeStruct((bsz, seq, GROUP_WIDTH), F32),
        grid=(bsz,),
        in_specs=[pl.BlockSpec((None, seq, MLA_IN_W), lambda b: (b, 0, 0)),
                  _resident(ct.shape), _resident(st.shape), _resident(gq.shape), _resident(gkv.shape),
                  _resident(wq.shape), _resident(wqs.shape), _resident(wkv.shape)],
        out_specs=pl.BlockSpec((None, seq, GROUP_WIDTH), lambda b: (b, 0, 0)),
        scratch_shapes=[pltpu.VMEM((seq, MLA_HEADS * LANE), BF16)] * 3,
        compiler_params=_cparams(("parallel",)), name="mla")(mla_in, ct, st, gq, gkv, wq, wqs, wkv)


def _hy_filter_kernel(feat_ref, w1_ref, b1_ref, fr_ref, w2_ref, b2_ref, w3_ref, decay_ref, o_ref):
    fr = fr_ref[...]
    h = jnp.sin(fr * (_dot_hi(feat_ref[...], w1_ref[...]) + b1_ref[...]))
    h = jnp.sin(fr * (_dot_hi(h, w2_ref[...]) + b2_ref[...]))
    decay = decay_ref[...]
    for k in range(2 * HY_ORDER):
        sl = slice(k * HY_WIDTH, (k + 1) * HY_WIDTH)
        o_ref[:, sl] = _dot_hi(h, w3_ref[:, sl]) * decay


def _hy_filter(feat, w1, b1, fr, w2, b2, w3, decay):
    seq = feat.shape[0]
    return pl.pallas_call(
        _hy_filter_kernel,
        out_shape=jax.ShapeDtypeStruct((seq, 2 * HY_ORDER * HY_WIDTH), F32),
        compiler_params=pltpu.CompilerParams(vmem_limit_bytes=VMEM_LIMIT),
        name="hy_filter")(feat, w1, b1, fr, w2, b2, w3, decay)


def _shift_rows(x, row, seq):
    prev = jnp.where(row == 0, 0.0, pltpu.roll(x, 1, 0))
    nxt = jnp.where(row == seq - 1, 0.0, pltpu.roll(x, seq - 1, 0))
    return prev, nxt


def _dwconv3(x, w, b, row, seq):
    prev, nxt = _shift_rows(x, row, seq)
    return prev * w[0:1] + x * w[1:2] + nxt * w[2:3] + b


def _hy_pre_kernel(u_ref, w_ref, b_ref, o_ref):
    seq = u_ref.shape[0]
    row = lax.broadcasted_iota(jnp.int32, (seq, HY_WIDTH), 0)
    for k in range(HY_ORDER + 1):
        sl = slice(k * HY_WIDTH, (k + 1) * HY_WIDTH)
        o_ref[:, sl] = _dwconv3(u_ref[:, sl], w_ref[:, sl], b_ref[:, sl], row, seq)


def _hy_pre(hy_u, w, b):
    bsz, seq, wd = hy_u.shape
    return pl.pallas_call(
        _hy_pre_kernel,
        out_shape=jax.ShapeDtypeStruct((bsz, seq, wd), F32),
        grid=(bsz,),
        in_specs=[pl.BlockSpec((None, seq, wd), lambda b: (b, 0, 0)), _resident(w.shape), _resident(b.shape)],
        out_specs=pl.BlockSpec((None, seq, wd), lambda b: (b, 0, 0)),
        compiler_params=_cparams(("parallel",)), name="hy_pre")(hy_u, w, b)


HY_ROWS16 = 16


def _hy_conv_kernel(bias_ref, v_ref, x1_ref, x2_ref, f_ref, o_ref, r_scr, z_scr, y_scr):
    bsz, seq = v_ref.shape
    nblk = seq // HY_BLK
    c = pl.program_id(0)

    def long_conv(z, order):
        k16 = jnp.broadcast_to(f_ref[order:order + 1, :], (HY_ROWS16, 2 * seq))
        k16 = pltpu.roll(k16, 0, 1, stride=1, stride_axis=0)
        for g in range(HY_BLK // HY_ROWS16):
            blk = k16 if g == 0 else pltpu.roll(k16, g * HY_ROWS16, 1)
            r_scr[g * HY_ROWS16:(g + 1) * HY_ROWS16, :] = blk.astype(BF16)
        for j in range(nblk):
            z_scr[j * bsz:(j + 1) * bsz, :] = z[:, j * HY_BLK:(j + 1) * HY_BLK].astype(BF16)
        for d in range(-(nblk - 1), nblk):
            lo_in, hi_in = max(0, -d), nblk - max(0, d)
            w_d = r_scr[:, seq + d * HY_BLK: seq + (d + 1) * HY_BLK]
            part = _dot(z_scr[lo_in * bsz:hi_in * bsz, :], w_d)
            rows = slice((lo_in + d) * bsz, (hi_in + d) * bsz)
            if d == -(nblk - 1):
                y_scr[...] = jnp.zeros_like(y_scr)
            y_scr[rows, :] += part
        return jnp.concatenate([y_scr[i * bsz:(i + 1) * bsz, :] for i in range(nblk)], axis=1)

    z0 = v_ref[...]
    z1 = x1_ref[...] * (long_conv(z0, 0) + bias_ref[0, c] * z0)
    o_ref[...] = x2_ref[...] * (long_conv(z1, 1) + bias_ref[1, c] * z1)


def _hy_conv(uct, filt, bias):
    _, bsz, seq = uct.shape
    chan = lambda k: pl.BlockSpec((None, bsz, seq), lambda c, k=k: (c + k * HY_WIDTH, 0, 0))
    return pl.pallas_call(
        _hy_conv_kernel,
        out_shape=jax.ShapeDtypeStruct((HY_WIDTH, bsz, seq), F32),
        grid=(HY_WIDTH,),
        in_specs=[pl.BlockSpec(memory_space=pltpu.SMEM), chan(0), chan(1), chan(2),
                  pl.BlockSpec((None, HY_ORDER, 2 * seq), lambda c: (c, 0, 0))],
        out_specs=pl.BlockSpec((None, bsz, seq), lambda c: (c, 0, 0)),
        scratch_shapes=[pltpu.VMEM((HY_BLK, 2 * seq), BF16),
                        pltpu.VMEM((seq // HY_BLK * bsz, HY_BLK), BF16),
                        pltpu.VMEM((seq // HY_BLK * bsz, HY_BLK), F32)],
        compiler_params=_cparams(("parallel",)), name="hy_conv")(bias, uct, uct, uct, filt)


def _swa_kernel(sink_ref, x_ref, o_ref, k_scr, v_scr):
    seq = x_ref.shape[0]
    nb = seq // BLOCK
    win = 3 * BLOCK
    grp = SWA_HEADS // SWA_KV_HEADS
    scale = SWA_HEAD_DIM ** -0.5
    kvw = SWA_KV_HEADS * LANE
    zeros = jnp.zeros((BLOCK, kvw), BF16)
    for scr, off in ((k_scr, 256), (v_scr, 512)):
        scr[0:BLOCK, :] = zeros
        scr[BLOCK + seq:2 * BLOCK + seq, :] = zeros
        scr[BLOCK:BLOCK + seq, :] = x_ref[:, off:off + kvw].astype(BF16)

    lane = lax.broadcasted_iota(jnp.int32, (1, LANE), 1)
    lo_half = lane < SWA_HEAD_DIM
    qi = lax.broadcasted_iota(jnp.int32, (grp * BLOCK, win), 0)
    ji = lax.broadcasted_iota(jnp.int32, (grp * BLOCK, win), 1)
    upper = qi >= BLOCK
    dist = jnp.abs(jnp.where(upper, qi - BLOCK, qi) + BLOCK - ji)
    in_band = dist <= SWA_WINDOW
    distf = dist.astype(F32)
    slopes = [(2.0 ** (-8.0 / SWA_HEADS)) ** (i + 1) for i in range(SWA_HEADS)]

    def block(n, carry):
        r0 = pl.multiple_of(n * BLOCK, BLOCK)
        kpos = r0 - BLOCK + ji
        valid = in_band & (kpos >= 0) & (kpos < seq)
        for j in range(SWA_KV_HEADS):
            sl = slice(j * LANE, (j + 1) * LANE)
            qp = x_ref[pl.ds(r0, BLOCK), sl] * scale
            q2 = jnp.concatenate([jnp.where(lo_half, qp, 0.0), jnp.where(lo_half, 0.0, qp)], 0).astype(BF16)
            kw = k_scr[pl.ds(r0, win), sl]
            vw = v_scr[pl.ds(r0, win), sl]
            sc = _dot_nt(q2, kw)
            slope = jnp.where(upper, slopes[grp * j + 1], slopes[grp * j])
            sc = jnp.where(valid, sc - slope * distf, NEG_INF)
            row_up = upper[:, 0:1]
            sk = jnp.where(row_up, sink_ref[grp * j + 1], sink_ref[grp * j])
            m = jnp.maximum(jnp.max(sc, -1, keepdims=True), sk)
            e = jnp.exp(sc - m)
            den = jnp.sum(e, -1, keepdims=True) + jnp.exp(sk - m)
            eb = e.astype(BF16)
            o_lo = _dot(eb[0:BLOCK], jnp.where(lo_half, vw, jnp.zeros_like(vw))) / den[0:BLOCK]
            o_hi = _dot(eb[BLOCK:], jnp.where(lo_half, jnp.zeros_like(vw), vw)) / den[BLOCK:]
            o_ref[pl.ds(r0, BLOCK), sl] = o_lo + o_hi
        return carry

    lax.fori_loop(0, nb, block, 0)


def _swa(swa_in, sink):
    bsz, seq, _ = swa_in.shape
    return pl.pallas_call(
        _swa_kernel,
        out_shape=jax.ShapeDtypeStruct((bsz, seq, GROUP_WIDTH), F32),
        grid=(bsz,),
        in_specs=[pl.BlockSpec(memory_space=pltpu.SMEM),
                  pl.BlockSpec((None, seq, SWA_IN_W), lambda b: (b, 0, 0))],
        out_specs=pl.BlockSpec((None, seq, GROUP_WIDTH), lambda b: (b, 0, 0)),
        scratch_shapes=[pltpu.VMEM((seq + 2 * BLOCK, SWA_KV_HEADS * LANE), BF16)] * 2,
        compiler_params=_cparams(("parallel",)), name="swa")(sink, swa_in)


def _softplus(x):
    return jnp.maximum(x, 0.0) + jnp.log1p(jnp.exp(-jnp.abs(x)))


def _ssd_kernel(x_ref, dtr_ref, cw_ref, cb_ref, dtb_row_ref, dtb_col_ref, alog_row_ref, alog_col_ref,
                dskip_ref, o_ref, act_scr, y_scr):
    seq = x_ref.shape[0]
    q = SSD_CHUNK
    nch = seq // q
    hpg = SSD_HEADS // SSD_GROUPS
    row = lax.broadcasted_iota(jnp.int32, (seq, GROUP_WIDTH), 0)
    for k in range(3):
        sl = slice(k * GROUP_WIDTH, (k + 1) * GROUP_WIDTH)
        xin = x_ref[:, 256 + k * GROUP_WIDTH:256 + (k + 1) * GROUP_WIDTH]
        act_scr[:, sl] = _silu(_dwconv3(xin, cw_ref[:, sl], cb_ref[:, sl], row, seq))

    ti = lax.broadcasted_iota(jnp.int32, (q, q), 0)
    si = lax.broadcasted_iota(jnp.int32, (q, q), 1)
    lower = si <= ti
    tri_l = jnp.where(lower, 1.0, 0.0).astype(F32)
    tri_u = jnp.where(si >= ti, 1.0, 0.0).astype(F32)
    lane = lax.broadcasted_iota(jnp.int32, (1, LANE), 1)
    first = lane < SSD_HEAD_DIM
    a_row = -jnp.exp(alog_row_ref[...])
    a_col = -jnp.exp(alog_col_ref[...])

    def per_head(col, h0):
        return jnp.where(first, col[:, h0:h0 + 1], col[:, h0 + 1:h0 + 2])

    def chunk_pass(direction):
        hb = direction * SSD_HEADS
        tri_col, tri_row = (tri_l, tri_u) if direction == 0 else (tri_u, tri_l)
        keep = lower if direction == 0 else (si >= ti)
        edge = q - 1 if direction == 0 else 0

        def chunk(i, carry):
            c = i if direction == 0 else nch - 1 - i
            r0 = pl.multiple_of(c * q, q)
            rows = pl.ds(r0, q)
            dt_col = _softplus(x_ref[rows, 1024:1024 + LANE] + dtb_row_ref[...])
            dt_row = _softplus(dtr_ref[c] + dtb_col_ref[...])
            cs_col = _dot_hi(tri_col, dt_col * a_row)
            cs_row = _dot_hi(dt_row * a_col, tri_row)
            new = []
            for g in range(SSD_GROUPS):
                gs = slice(g * LANE, (g + 1) * LANE)
                h0 = hb + g * hpg
                xg = act_scr[rows, gs]
                bg = act_scr[rows, GROUP_WIDTH + g * LANE:GROUP_WIDTH + (g + 1) * LANE]
                cg = act_scr[rows, 2 * GROUP_WIDTH + g * LANE:2 * GROUP_WIDTH + (g + 1) * LANE]
                xdt = xg * per_head(dt_col, h0)
                cb = _dot_nt(cg.astype(BF16), bg.astype(BF16))
                y = None
                for r in range(hpg):
                    seg = cs_col[:, h0 + r:h0 + r + 1] - cs_row[h0 + r:h0 + r + 1, :]
                    lmat = jnp.exp(jnp.where(keep, seg, NEG_INF))
                    xh = jnp.where(first if r == 0 else ~first, xdt, 0.0)
                    part = _dot((cb * lmat).astype(BF16), xh.astype(BF16))
                    y = part if y is None else y + part
                cs_g = per_head(cs_col, h0)
                total = cs_g[edge:edge + 1, :]
                y = y + _dot(cg.astype(BF16), carry[g].astype(BF16)) * jnp.exp(cs_g)
                st = _dot(bg.T.astype(BF16), (xdt * jnp.exp(total - cs_g)).astype(BF16))
                new.append(carry[g] * jnp.exp(total) + st)
                if direction == 0:
                    y_scr[rows, gs] = y
                else:
                    dsum = dskip_ref[0:1, gs] + dskip_ref[1:2, gs]
                    z = x_ref[rows, gs]
                    o_ref[rows, gs] = (y_scr[rows, gs] + y + dsum * xg) * _silu(z)
            return tuple(new)

        init = tuple(jnp.zeros((SSD_STATE, LANE), F32) for _ in range(SSD_GROUPS))
        lax.fori_loop(0, nch, chunk, init)

    chunk_pass(0)
    chunk_pass(1)


def _ssd(ssd_in, dt_rows, cw, cb, dtb_row, dtb_col, alog_row, alog_col, dskip):
    bsz, seq, _ = ssd_in.shape
    nch = seq // SSD_CHUNK
    small = (cw, cb, dtb_row, dtb_col, alog_row, alog_col, dskip)
    return pl.pallas_call(
        _ssd_kernel,
        out_shape=jax.ShapeDtypeStruct((bsz, seq, GROUP_WIDTH), F32),
        grid=(bsz,),
        in_specs=[pl.BlockSpec((None, seq, SSD_IN_W), lambda b: (b, 0, 0)),
                  pl.BlockSpec((None, nch, 2 * SSD_HEADS, SSD_CHUNK), lambda b: (b, 0, 0, 0))]
                 + [_resident(a.shape) for a in small],
        out_specs=pl.BlockSpec((None, seq, GROUP_WIDTH), lambda b: (b, 0, 0)),
        scratch_shapes=[pltpu.VMEM((seq, 3 * GROUP_WIDTH), F32), pltpu.VMEM((seq, GROUP_WIDTH), F32)],
        compiler_params=_cparams(("parallel",)), name="ssd")(ssd_in, dt_rows, *small)


def _out_proj_kernel(ya_ref, yb_ref, yc_ref, yd_ref, h_ref, g_ref, w_ref, lg_ref, lb_ref, o_ref):
    parts = []
    for k, ref in enumerate((ya_ref, yb_ref, yc_ref, yd_ref)):
        parts.append(_rms(ref[...], g_ref[:, k * GROUP_WIDTH:(k + 1) * GROUP_WIDTH]).astype(BF16))
    y = jnp.concatenate(parts, axis=1)
    o_ref[...] = _layer_norm(ALPHA * h_ref[...] + _dot(y, w_ref[...]), lg_ref[...], lb_ref[...])


def _out_proj(ya, yb, yc, yd, h, g, w, lg, lb):
    n = h.shape[0]
    row = lambda wd: pl.BlockSpec((ROW_TILE, wd), lambda i: (i, 0))
    return pl.pallas_call(
        _out_proj_kernel,
        out_shape=jax.ShapeDtypeStruct((n, D_MODEL), F32),
        grid=(n // ROW_TILE,),
        in_specs=[row(GROUP_WIDTH)] * 4 + [row(D_MODEL), _resident(g.shape), _resident(w.shape),
                                           _resident(lg.shape), _resident(lb.shape)],
        out_specs=row(D_MODEL),
        compiler_params=_cparams(("parallel",)), name="out_proj")(ya, yb, yc, yd, h, g, w, lg, lb)


def _ffn_kernel(tiles_per_seq, hp_ref, h_ref, hn_ref, wg_ref, wu_ref, cw_ref, cb_ref, wd_ref,
                lg_ref, lb_ref, o_ref):
    i = pl.program_id(0)
    ext = ROW_TILE + 2 * FFN_HALO
    h = h_ref[...]
    hb = h.astype(BF16)
    hext = jnp.concatenate([hp_ref[...].astype(BF16), hb, hn_ref[...].astype(BF16)], axis=0)
    row = lax.broadcasted_iota(jnp.int32, (ROW_TILE, 1), 0)
    keep_prev = jnp.where((row == 0) & (i % tiles_per_seq == 0), 0.0, 1.0)
    keep_next = jnp.where((row == ROW_TILE - 1) & (i % tiles_per_seq == tiles_per_seq - 1), 0.0, 1.0)
    f = None
    for c0 in range(0, D_FF, FFN_CHUNK):
        sl = slice(c0, c0 + FFN_CHUNK)
        gate = _dot(hext, wg_ref[:, sl])
        g_prev = pltpu.roll(gate, 1, 0)[FFN_HALO:FFN_HALO + ROW_TILE]
        g_next = pltpu.roll(gate, ext - 1, 0)[FFN_HALO:FFN_HALO + ROW_TILE]
        g_cur = gate[FFN_HALO:FFN_HALO + ROW_TILE]
        conv = (g_prev * keep_prev * cw_ref[0:1, sl] + g_cur * cw_ref[1:2, sl]
                + g_next * keep_next * cw_ref[2:3, sl] + cb_ref[:, sl])
        act = (_silu(conv) * _dot(hb, wu_ref[:, sl])).astype(BF16)
        part = _dot(act, wd_ref[sl, :])
        f = part if f is None else f + part
    o_ref[...] = _layer_norm(ALPHA * h + f, lg_ref[...], lb_ref[...])


def _ffn(h, seq, wg, wu, cw, cb, wd, lg, lb):
    n = h.shape[0]
    per_tile = ROW_TILE // FFN_HALO
    last = n // FFN_HALO - 1
    small = (wg, wu, cw, cb, wd, lg, lb)
    return pl.pallas_call(
        functools.partial(_ffn_kernel, seq // ROW_TILE),
        out_shape=jax.ShapeDtypeStruct((n, D_MODEL), F32),
        grid=(n // ROW_TILE,),
        in_specs=[pl.BlockSpec((FFN_HALO, D_MODEL), lambda i: (jnp.maximum(i * per_tile - 1, 0), 0)),
                  pl.BlockSpec((ROW_TILE, D_MODEL), lambda i: (i, 0)),
                  pl.BlockSpec((FFN_HALO, D_MODEL), lambda i: (jnp.minimum((i + 1) * per_tile, last), 0))]
                 + [_resident(a.shape) for a in small],
        out_specs=pl.BlockSpec((ROW_TILE, D_MODEL), lambda i: (i, 0)),
        compiler_params=_cparams(("parallel",)), name="ffn")(h, h, h, *small)


def _ple_kernel(h_ref, p_ref, wp_ref, wg_ref, bg_ref, lg_ref, lb_ref, o_ref):
    h = h_ref[...]
    gate = _dot(h.astype(BF16), wg_ref[...]) + bg_ref[...]
    e = _dot(p_ref[...].astype(BF16), wp_ref[...]) / (1.0 + jnp.exp(-gate))
    o_ref[...] = _layer_norm(ALPHA * h + e, lg_ref[...], lb_ref[...])


def _ple(h, p, wp, wg, bg, lg, lb):
    n = h.shape[0]
    row = lambda wd: pl.BlockSpec((ROW_TILE, wd), lambda i: (i, 0))
    small = (wp, wg, bg, lg, lb)
    return pl.pallas_call(
        _ple_kernel,
        out_shape=jax.ShapeDtypeStruct((n, D_MODEL), F32),
        grid=(n // ROW_TILE,),
        in_specs=[row(D_MODEL), row(PLE_DIM)] + [_resident(a.shape) for a in small],
        out_specs=row(D_MODEL),
        compiler_params=_cparams(("parallel",)), name="ple")(h, p, *small)


def _pad_cols(w, width):
    return jnp.pad(w, ((0, 0), (0, width - w.shape[1])))


def _in_proj_weight(w):
    o = np.cumsum([0, 256, 128, 32, 768, 256, 128, 128, 256, 768, 8])
    cq, ckv, kr, hy, sq, sk, sv, z, xbc, dt = (w[:, o[i]:o[i + 1]] for i in range(10))
    zc = lambda n: jnp.zeros((w.shape[0], n), w.dtype)
    half = MLA_ROPE // 2
    kr_sw = jnp.concatenate([kr[:, half:], kr[:, :half]], 1)
    dup = lambda t: jnp.concatenate([t[:, :64], t[:, :64], t[:, 64:], t[:, 64:]], 1)
    cols = [cq, ckv, zc(64), kr, zc(32), zc(64), kr_sw, zc(32), hy, sq, dup(sk), dup(sv), z, xbc, dt, zc(120)]
    return jnp.concatenate(cols, 1).astype(BF16)


def _mla_weights(w_uq, w_ukv):
    r = w_uq.shape[0]
    half = MLA_ROPE // 2
    qh = w_uq.reshape(r, MLA_HEADS, MLA_NOPE + MLA_ROPE)
    rope = qh[..., MLA_NOPE:]
    rope_sw = jnp.concatenate([rope[..., half:], rope[..., :half]], -1)
    z = lambda n: jnp.zeros((r, MLA_HEADS, n), w_uq.dtype)
    wq = jnp.concatenate([qh, z(LANE - MLA_NOPE - MLA_ROPE)], -1).reshape(r, MLA_HEADS * LANE)
    wqs = jnp.concatenate([z(MLA_NOPE), rope_sw, z(LANE - MLA_NOPE - MLA_ROPE)], -1).reshape(r, MLA_HEADS * LANE)
    rk = w_ukv.shape[0]
    kvh = w_ukv.reshape(rk, MLA_HEADS, MLA_NOPE + MLA_V)
    zk = jnp.zeros((rk, MLA_HEADS, LANE - MLA_NOPE), w_ukv.dtype)
    wk = jnp.concatenate([kvh[..., :MLA_NOPE], zk], -1).reshape(rk, MLA_HEADS * LANE)
    v = kvh[..., MLA_NOPE:]
    zv = jnp.zeros_like(v)
    odd = (jnp.arange(MLA_HEADS) % 2 == 1)[None, :, None]
    wv = jnp.concatenate([jnp.where(odd, zv, v), jnp.where(odd, v, zv)], -1).reshape(rk, MLA_HEADS * LANE)
    return wq.astype(BF16), wqs.astype(BF16), jnp.concatenate([wk, wv], 1).astype(BF16)


def _rope_tables(seq):
    inv_freq = ROPE_THETA ** (-jnp.arange(0, MLA_ROPE, 2, dtype=F32) / MLA_ROPE)
    ang = jnp.arange(seq, dtype=F32)[:, None] * inv_freq[None, :]
    cos, sin = jnp.cos(ang), jnp.sin(ang)
    one = jnp.ones((seq, MLA_NOPE), F32)
    zero = lambda n: jnp.zeros((seq, n), F32)
    ct = jnp.concatenate([one, cos, cos, zero(LANE - MLA_NOPE - MLA_ROPE)], 1)
    st = jnp.concatenate([zero(MLA_NOPE), -sin, sin, zero(LANE - MLA_NOPE - MLA_ROPE)], 1)
    return ct, st


def _hy_tables(seq):
    t = jnp.linspace(0.0, 1.0, seq, dtype=F32)[:, None]
    ang = 2.0 * math.pi * jnp.arange(seq, dtype=F32)[:, None] / seq
    bands = jnp.linspace(1e-4, HY_BANDS - 1, HY_BANDS, dtype=F32)[None, :]
    feat = jnp.concatenate([t, jnp.cos(bands * ang), -jnp.sin(bands * ang)], -1)
    max_decay = math.log(HY_TARGET) / HY_FAST_DECAY
    min_decay = math.log(HY_TARGET) / HY_SLOW_DECAY
    deltas = jnp.linspace(min_decay, max_decay, HY_WIDTH, dtype=F32)
    decay = jnp.exp(-t * jnp.abs(deltas)[None, :])
    return _pad_cols(feat, LANE), decay


def _pad2(w, rows, cols):
    return jnp.pad(w, ((0, rows - w.shape[0]), (0, cols - w.shape[1])))


def _hy_lag_filters(filt, seq):
    hf = filt.reshape(seq, HY_ORDER, 2, HY_WIDTH)
    fwd, bwd = hf[:, :, 0], hf[:, :, 1]
    k2 = jnp.concatenate([jnp.zeros_like(bwd[:1]), bwd[:0:-1], fwd], 0)
    return jnp.transpose(k2, (2, 1, 0))


def kernel(x, p, emb_ln_g, emb_ln_b, w_in, mla_q_norm, mla_kv_norm, mla_w_uq, mla_w_ukv, hy_conv_w, hy_conv_b, hy_f_w1, hy_f_b1, hy_f_freq, hy_f_w2, hy_f_b2, hy_f_w3, hy_bias, swa_sink, ssd_conv_w, ssd_conv_b, ssd_dt_bias, ssd_a_log, ssd_d, mix_norm_g, w_out, ln1_g, ln1_b, ffn_w_gate, ffn_w_up, ffn_conv_w, ffn_conv_b, ffn_w_down, ln2_g, ln2_b, ple_w_proj, ple_w_gate, ple_b_gate, ln3_g, ln3_b):
    bsz, seq, d = x.shape
    n = bsz * seq
    row = lambda v: v.reshape(1, -1)
    ct, st = _rope_tables(seq)
    feat, decay = _hy_tables(seq)
    h = x.reshape(n, d)
    for i in range(DEPTH):
        w_in_i = _in_proj_weight(w_in[i])
        if i == 0:
            h, mla_in, hy_u, swa_in, ssd_in = _in_proj(h, w_in_i, (row(emb_ln_g), row(emb_ln_b)))
        else:
            mla_in, hy_u, swa_in, ssd_in = _in_proj(h, w_in_i)

        wq, wqs, wkv = _mla_weights(mla_w_uq[i], mla_w_ukv[i])
        y_a = _mla(mla_in.reshape(bsz, seq, MLA_IN_W), ct, st, row(mla_q_norm[i]), row(mla_kv_norm[i]),
                   wq, wqs, wkv)

        filt = _hy_filter(feat, _pad2(hy_f_w1[i], LANE, LANE), _pad_cols(row(hy_f_b1[i]), LANE),
                          _pad_cols(row(hy_f_freq[i]), LANE), _pad2(hy_f_w2[i], LANE, LANE),
                          _pad_cols(row(hy_f_b2[i]), LANE), _pad2(hy_f_w3[i], LANE, 2 * HY_ORDER * HY_WIDTH),
                          decay)
        uc = _hy_pre(hy_u.reshape(bsz, seq, HY_IN_W), hy_conv_w[i], row(hy_conv_b[i]))
        y_bt = _hy_conv(jnp.transpose(uc, (2, 0, 1)), _hy_lag_filters(filt, seq), hy_bias[i])
        y_b = jnp.transpose(y_bt, (1, 2, 0))

        y_c = _swa(swa_in.reshape(bsz, seq, SWA_IN_W), swa_sink[i])

        ssd3 = ssd_in.reshape(bsz, seq, SSD_IN_W)
        dt_rows = jnp.transpose(
            ssd3[:, :, 1024:1024 + 2 * SSD_HEADS].reshape(bsz, seq // SSD_CHUNK, SSD_CHUNK, 2 * SSD_HEADS),
            (0, 1, 3, 2))
        dtb = ssd_dt_bias[i].reshape(-1)
        alog = ssd_a_log[i].reshape(-1)
        dskip = jnp.repeat(ssd_d[i], SSD_HEAD_DIM, axis=1)
        y_d = _ssd(ssd3, dt_rows, ssd_conv_w[i], row(ssd_conv_b[i]), _pad_cols(row(dtb), LANE), dtb.reshape(-1, 1),
                   _pad_cols(row(alog), LANE), alog.reshape(-1, 1), dskip)

        flat = lambda y: y.reshape(n, GROUP_WIDTH)
        h = _out_proj(flat(y_a), flat(y_b), flat(y_c), flat(y_d), h, row(mix_norm_g[i]),
                      w_out[i].astype(BF16), row(ln1_g[i]), row(ln1_b[i]))
        h = _ffn(h, seq, ffn_w_gate[i].astype(BF16), ffn_w_up[i].astype(BF16), ffn_conv_w[i],
                 row(ffn_conv_b[i]), ffn_w_down[i].astype(BF16), row(ln2_g[i]), row(ln2_b[i]))
        h = _ple(h, p[i].reshape(n, PLE_DIM), ple_w_proj[i].astype(BF16), ple_w_gate[i].astype(BF16),
                 row(ple_b_gate[i]), row(ln3_g[i]), row(ln3_b[i]))
    return h.reshape(bsz, seq, d)
```

```python
import functools
import math

import jax
import jax.numpy as jnp
import numpy as np
from jax import lax
from jax.experimental import pallas as pl
from jax.experimental.pallas import tpu as pltpu

F32 = jnp.float32
BF16 = jnp.bfloat16

D_MODEL = 1024
DEPTH = 2
PLE_DIM = 256
GROUP_WIDTH = 256
BLOCK = 128

MLA_HEADS = 4
MLA_Q_RANK = 256
MLA_KV_RANK = 128
MLA_NOPE = 64
MLA_ROPE = 32
MLA_V = 64
ROPE_THETA = 10000.0

HY_WIDTH = 256
HY_ORDER = 2
HY_EMB = 33
HY_BANDS = 16
HY_FAST_DECAY = 0.3
HY_SLOW_DECAY = 1.5
HY_TARGET = 1e-2
HY_BLK = 256

SWA_HEADS = 4
SWA_KV_HEADS = 2
SWA_HEAD_DIM = 64
SWA_WINDOW = 128

SSD_D_INNER = 256
SSD_HEAD_DIM = 64
SSD_HEADS = 4
SSD_GROUPS = 2
SSD_STATE = 128
SSD_CHUNK = 128

D_FF = 2816
LN_EPS = 1e-5
RMS_EPS = 1e-6
NEG_INF = -1e30
ALPHA = (2.0 * DEPTH) ** 0.25

LANE = 128
VMEM_LIMIT = 56 * 1024 * 1024

MLA_IN_W = 384
MISC_W = 128
HY_IN_W = 768
SWA_IN_W = 512
SSD_IN_W = 1024
IN_WIDTHS = (MLA_IN_W, MISC_W, HY_IN_W, SWA_IN_W, SSD_IN_W)
KR_LANE = MLA_NOPE

ROW_TILE = 512
FFN_HALO = 16
FFN_CHUNK = 1408


def _cparams(sem):
    return pltpu.CompilerParams(dimension_semantics=sem, vmem_limit_bytes=VMEM_LIMIT)


def _resident(shape):
    nd = len(shape)
    return pl.BlockSpec(shape, lambda *_: (0,) * nd, pipeline_mode=pl.Buffered(1))


def _layer_norm(x, g, b):
    mu = jnp.mean(x, -1, keepdims=True)
    xc = x - mu
    var = jnp.mean(xc * xc, -1, keepdims=True)
    return xc * lax.rsqrt(var + LN_EPS) * g + b


def _rms(x, g):
    return x * lax.rsqrt(jnp.mean(x * x, -1, keepdims=True) + RMS_EPS) * g


def _silu(x):
    return x / (1.0 + jnp.exp(-x))


def _dot(a, b):
    return jnp.dot(a, b, preferred_element_type=F32)


def _dot_nt(a, b):
    return lax.dot_general(a, b, (((1,), (1,)), ((), ())), preferred_element_type=F32)


def _dot_hi(a, b):
    return jnp.dot(a, b, preferred_element_type=F32, precision=lax.Precision.HIGHEST)


def _in_proj_kernel(apply_ln, *refs):
    if apply_ln:
        x_ref, g_ref, b_ref, w_ref, h_ref = refs[:5]
        out_refs = refs[5:]
        h = _layer_norm(x_ref[...], g_ref[...], b_ref[...])
        h_ref[...] = h
    else:
        x_ref, w_ref = refs[:2]
        out_refs = refs[2:]
        h = x_ref[...]
    u = _dot(h.astype(BF16), w_ref[...])
    o = 0
    for ref, w in zip(out_refs, IN_WIDTHS):
        ref[...] = u[:, o:o + w]
        o += w


def _in_proj(x, w, ln=None):
    n = x.shape[0]
    row = lambda wd: pl.BlockSpec((ROW_TILE, wd), lambda i: (i, 0))
    outs = [jax.ShapeDtypeStruct((n, wd), F32) for wd in IN_WIDTHS]
    out_specs = [row(wd) for wd in IN_WIDTHS]
    if ln is not None:
        args = (x, ln[0], ln[1], w)
        in_specs = [row(D_MODEL), _resident((1, D_MODEL)), _resident((1, D_MODEL)), _resident(w.shape)]
        outs = [jax.ShapeDtypeStruct((n, D_MODEL), F32)] + outs
        out_specs = [row(D_MODEL)] + out_specs
    else:
        args = (x, w)
        in_specs = [row(D_MODEL), _resident(w.shape)]
    return pl.pallas_call(
        functools.partial(_in_proj_kernel, ln is not None),
        out_shape=outs, grid=(n // ROW_TILE,), in_specs=in_specs, out_specs=out_specs,
        compiler_params=_cparams(("parallel",)), name="in_proj")(*args)


MLA_TQ = 256


def _mla_kernel(x_ref, misc_ref, ct_ref, st_ref, gq_ref, gkv_ref, wq_ref, wqs_ref, wkv_ref, o_ref,
                q_scr, k_scr, v_scr):
    seq = x_ref.shape[0]
    scale = (MLA_NOPE + MLA_ROPE) ** -0.5
    half = MLA_ROPE // 2
    ct = ct_ref[...]
    st = st_ref[...]
    cqn = _rms(x_ref[:, 0:256], gq_ref[...]).astype(BF16)
    ckvn = _rms(x_ref[:, 256:384], gkv_ref[...]).astype(BF16)
    lane = lax.broadcasted_iota(jnp.int32, (1, LANE), 1)
    kr = misc_ref[...]
    kr_swapped = jnp.where(lane < KR_LANE + half, pltpu.roll(kr, LANE - half, 1), pltpu.roll(kr, half, 1))
    on_rope = (lane >= KR_LANE) & (lane < KR_LANE + MLA_ROPE)
    k_rope = jnp.where(on_rope, kr * ct + kr_swapped * st, 0.0)
    for h in range(MLA_HEADS):
        sl = slice(h * LANE, (h + 1) * LANE)
        q = _dot(cqn, wq_ref[:, sl]) * ct + _dot(cqn, wqs_ref[:, sl]) * st
        q_scr[:, sl] = (q * scale).astype(BF16)
        k_scr[:, sl] = (_dot(ckvn, wkv_ref[:, sl]) + k_rope).astype(BF16)
        vsl = slice(MLA_HEADS * LANE + h * LANE, MLA_HEADS * LANE + (h + 1) * LANE)
        v_scr[:, sl] = _dot(ckvn, wkv_ref[:, vsl]).astype(BF16)

    def q_tile(t, carry):
        rows = pl.ds(pl.multiple_of(t * MLA_TQ, MLA_TQ), MLA_TQ)
        for pair in range(MLA_HEADS // 2):
            acc = None
            for h in (2 * pair, 2 * pair + 1):
                sl = slice(h * LANE, (h + 1) * LANE)
                s = _dot_nt(q_scr[rows, sl], k_scr[:, sl])
                m = jnp.max(s, -1, keepdims=True)
                e = jnp.exp(s - m)
                l = jnp.sum(e, -1, keepdims=True)
                o = _dot(e.astype(BF16), v_scr[:, sl]) / l
                acc = o if acc is None else acc + o
            o_ref[rows, pair * LANE:(pair + 1) * LANE] = acc
        return carry

    lax.fori_loop(0, seq // MLA_TQ, q_tile, 0)


def _mla(mla_in, misc, ct, st, gq, gkv, wq, wqs, wkv):
    bsz, seq, _ = mla_in.shape
    return pl.pallas_call(
        _mla_kernel,
        out_shape=jax.ShapeDtypeStruct((bsz, seq, GROUP_WIDTH), F32),
        grid=(bsz,),
        in_specs=[pl.BlockSpec((None, seq, MLA_IN_W), lambda b: (b, 0, 0)),
                  pl.BlockSpec((None, seq, MISC_W), lambda b: (b, 0, 0)),
                  _resident(ct.shape), _resident(st.shape), _resident(gq.shape), _resident(gkv.shape),
                  _resident(wq.shape), _resident(wqs.shape), _resident(wkv.shape)],
        out_specs=pl.BlockSpec((None, seq, GROUP_WIDTH), lambda b: (b, 0, 0)),
        scratch_shapes=[pltpu.VMEM((seq, MLA_HEADS * LANE), BF16)] * 3,
        compiler_params=_cparams(("parallel",)), name="mla")(mla_in, misc, ct, st, gq, gkv, wq, wqs, wkv)


def _hy_filter_kernel(feat_ref, w1_ref, b1_ref, fr_ref, w2_ref, b2_ref, w3_ref, decay_ref, o_ref):
    fr = fr_ref[...]
    h = jnp.sin(fr * (_dot_hi(feat_ref[...], w1_ref[...]) + b1_ref[...]))
    h = jnp.sin(fr * (_dot_hi(h, w2_ref[...]) + b2_ref[...]))
    decay = decay_ref[...]
    for k in range(2 * HY_ORDER):
        sl = slice(k * HY_WIDTH, (k + 1) * HY_WIDTH)
        o_ref[:, sl] = _dot_hi(h, w3_ref[:, sl]) * decay


def _hy_filter(feat, w1, b1, fr, w2, b2, w3, decay):
    seq = feat.shape[0]
    return pl.pallas_call(
        _hy_filter_kernel,
        out_shape=jax.ShapeDtypeStruct((seq, 2 * HY_ORDER * HY_WIDTH), F32),
        compiler_params=pltpu.CompilerParams(vmem_limit_bytes=VMEM_LIMIT),
        name="hy_filter")(feat, w1, b1, fr, w2, b2, w3, decay)


def _shift_rows(x, row, seq):
    prev = jnp.where(row == 0, 0.0, pltpu.roll(x, 1, 0))
    nxt = jnp.where(row == seq - 1, 0.0, pltpu.roll(x, seq - 1, 0))
    return prev, nxt


def _dwconv3(x, w, b, row, seq):
    prev, nxt = _shift_rows(x, row, seq)
    return prev * w[0:1] + x * w[1:2] + nxt * w[2:3] + b


def _hy_pre_kernel(u_ref, w_ref, b_ref, o_ref):
    seq = u_ref.shape[0]
    row = lax.broadcasted_iota(jnp.int32, (seq, HY_WIDTH), 0)
    for k in range(HY_ORDER + 1):
        sl = slice(k * HY_WIDTH, (k + 1) * HY_WIDTH)
        o_ref[sl, :] = _dwconv3(u_ref[:, sl], w_ref[:, sl], b_ref[:, sl], row, seq).T


def _hy_pre(hy_u, w, b):
    bsz, seq, wd = hy_u.shape
    return pl.pallas_call(
        _hy_pre_kernel,
        out_shape=jax.ShapeDtypeStruct((wd, bsz * seq), F32),
        grid=(bsz,),
        in_specs=[pl.BlockSpec((None, seq, wd), lambda b: (b, 0, 0)), _resident(w.shape), _resident(b.shape)],
        out_specs=pl.BlockSpec((wd, seq), lambda b: (0, b)),
        compiler_params=_cparams(("parallel",)), name="hy_pre")(hy_u, w, b)


HY_ROWS16 = 16
HY_CH = 4


def _hy_conv_kernel(bias_ref, v_ref, x1_ref, x2_ref, f_ref, o_ref, r_scr):
    _, bsz, seq = v_ref.shape
    nblk = seq // HY_BLK
    c0 = pl.program_id(0) * HY_CH

    def build_toeplitz(slot, taps):
        k16 = jnp.broadcast_to(taps, (HY_ROWS16, 2 * seq))
        k16 = pltpu.roll(k16, 0, 1, stride=1, stride_axis=0)
        for g in range(LANE // HY_ROWS16):
            blk = (k16 if g == 0 else pltpu.roll(k16, g * HY_ROWS16, 1)).astype(BF16)
            r_scr[slot, g * HY_ROWS16:(g + 1) * HY_ROWS16, :] = blk
            r_scr[slot, LANE + g * HY_ROWS16:LANE + (g + 1) * HY_ROWS16, LANE:] = blk[:, :2 * seq - LANE]

    def long_conv(z, slot):
        zr = jnp.concatenate([z[:, j * HY_BLK:(j + 1) * HY_BLK] for j in range(nblk)], axis=0).astype(BF16)
        acc = [None] * nblk
        for d in range(-(nblk - 1), nblk):
            lo_in, hi_in = max(0, -d), nblk - max(0, d)
            w_d = r_scr[slot, :, seq + d * HY_BLK: seq + (d + 1) * HY_BLK]
            part = _dot(zr[lo_in * bsz:hi_in * bsz, :], w_d)
            for j in range(lo_in, hi_in):
                piece = part[(j - lo_in) * bsz:(j - lo_in + 1) * bsz, :]
                acc[j + d] = piece if acc[j + d] is None else acc[j + d] + piece
        return jnp.concatenate(acc, axis=1)

    for ch in range(HY_CH):
        for order in range(HY_ORDER):
            build_toeplitz(ch * HY_ORDER + order, f_ref[ch, order:order + 1, :])
    for ch in range(HY_CH):
        z0 = v_ref[ch]
        z1 = x1_ref[ch] * (long_conv(z0, ch * HY_ORDER) + bias_ref[0, c0 + ch] * z0)
        o_ref[ch] = x2_ref[ch] * (long_conv(z1, ch * HY_ORDER + 1) + bias_ref[1, c0 + ch] * z1)


def _hy_conv(uct, filt, bias):
    _, bsz, seq = uct.shape
    per_slab = HY_WIDTH // HY_CH
    chan = lambda k: pl.BlockSpec((HY_CH, bsz, seq), lambda c, k=k: (c + k * per_slab, 0, 0))
    return pl.pallas_call(
        _hy_conv_kernel,
        out_shape=jax.ShapeDtypeStruct((HY_WIDTH, bsz, seq), F32),
        grid=(per_slab,),
        in_specs=[pl.BlockSpec(memory_space=pltpu.SMEM), chan(0), chan(1), chan(2),
                  pl.BlockSpec((HY_CH, HY_ORDER, 2 * seq), lambda c: (c, 0, 0))],
        out_specs=pl.BlockSpec((HY_CH, bsz, seq), lambda c: (c, 0, 0)),
        scratch_shapes=[pltpu.VMEM((HY_CH * HY_ORDER, HY_BLK, 2 * seq), BF16)],
        compiler_params=_cparams(("parallel",)), name="hy_conv")(bias, uct, uct, uct, filt)


def _swa_kernel(sink_ref, x_ref, o_ref, k_scr, v_scr):
    seq = x_ref.shape[0]
    nb = seq // BLOCK
    win = 3 * BLOCK
    grp = SWA_HEADS // SWA_KV_HEADS
    scale = SWA_HEAD_DIM ** -0.5
    kvw = SWA_KV_HEADS * LANE
    zeros = jnp.zeros((BLOCK, kvw), BF16)
    lane = lax.broadcasted_iota(jnp.int32, (1, LANE), 1)
    lo_half = lane < SWA_HEAD_DIM
    for scr, off in ((k_scr, 256), (v_scr, 256 + LANE)):
        scr[0:BLOCK, :] = zeros
        scr[BLOCK + seq:2 * BLOCK + seq, :] = zeros
        both = x_ref[:, off:off + LANE]
        flipped = pltpu.roll(both, SWA_HEAD_DIM, 1)
        scr[BLOCK:BLOCK + seq, 0:LANE] = jnp.where(lo_half, both, flipped).astype(BF16)
        scr[BLOCK:BLOCK + seq, LANE:2 * LANE] = jnp.where(lo_half, flipped, both).astype(BF16)

    qi = lax.broadcasted_iota(jnp.int32, (grp * BLOCK, win), 0)
    ji = lax.broadcasted_iota(jnp.int32, (grp * BLOCK, win), 1)
    upper = qi >= BLOCK
    dist = jnp.abs(jnp.where(upper, qi - BLOCK, qi) + BLOCK - ji)
    in_band = dist <= SWA_WINDOW
    distf = dist.astype(F32)
    slopes = [(2.0 ** (-8.0 / SWA_HEADS)) ** (i + 1) for i in range(SWA_HEADS)]

    def block(n, carry):
        r0 = pl.multiple_of(n * BLOCK, BLOCK)
        kpos = r0 - BLOCK + ji
        valid = in_band & (kpos >= 0) & (kpos < seq)
        for j in range(SWA_KV_HEADS):
            sl = slice(j * LANE, (j + 1) * LANE)
            qp = x_ref[pl.ds(r0, BLOCK), sl] * scale
            q2 = jnp.concatenate([jnp.where(lo_half, qp, 0.0), jnp.where(lo_half, 0.0, qp)], 0).astype(BF16)
            kw = k_scr[pl.ds(r0, win), sl]
            vw = v_scr[pl.ds(r0, win), sl]
            sc = _dot_nt(q2, kw)
            slope = jnp.where(upper, slopes[grp * j + 1], slopes[grp * j])
            sc = jnp.where(valid, sc - slope * distf, NEG_INF)
            row_up = upper[:, 0:1]
            sk = jnp.where(row_up, sink_ref[grp * j + 1], sink_ref[grp * j])
            m = jnp.maximum(jnp.max(sc, -1, keepdims=True), sk)
            e = jnp.exp(sc - m)
            den = jnp.sum(e, -1, keepdims=True) + jnp.exp(sk - m)
            eb = e.astype(BF16)
            o_lo = _dot(eb[0:BLOCK], jnp.where(lo_half, vw, jnp.zeros_like(vw))) / den[0:BLOCK]
            o_hi = _dot(eb[BLOCK:], jnp.where(lo_half, jnp.zeros_like(vw), vw)) / den[BLOCK:]
            o_ref[pl.ds(r0, BLOCK), sl] = o_lo + o_hi
        return carry

    lax.fori_loop(0, nb, block, 0)


def _swa(swa_in, sink):
    bsz, seq, _ = swa_in.shape
    return pl.pallas_call(
        _swa_kernel,
        out_shape=jax.ShapeDtypeStruct((bsz, seq, GROUP_WIDTH), F32),
        grid=(bsz,),
        in_specs=[pl.BlockSpec(memory_space=pltpu.SMEM),
                  pl.BlockSpec((None, seq, SWA_IN_W), lambda b: (b, 0, 0))],
        out_specs=pl.BlockSpec((None, seq, GROUP_WIDTH), lambda b: (b, 0, 0)),
        scratch_shapes=[pltpu.VMEM((seq + 2 * BLOCK, SWA_KV_HEADS * LANE), BF16)] * 2,
        compiler_params=_cparams(("parallel",)), name="swa")(sink, swa_in)


def _softplus(x):
    return jnp.maximum(x, 0.0) + jnp.log1p(jnp.exp(-jnp.abs(x)))


def _ssd_kernel(x_ref, misc_ref, dtr_ref, cw_ref, cb_ref, dtb_row_ref, dtb_col_ref, alog_row_ref, alog_col_ref,
                dskip_ref, o_ref, act_scr, yf_scr, yb_scr):
    seq = x_ref.shape[0]
    q = SSD_CHUNK
    nch = seq // q
    hpg = SSD_HEADS // SSD_GROUPS
    row = lax.broadcasted_iota(jnp.int32, (seq, GROUP_WIDTH), 0)
    for k in range(3):
        sl = slice(k * GROUP_WIDTH, (k + 1) * GROUP_WIDTH)
        xin = x_ref[:, 256 + k * GROUP_WIDTH:256 + (k + 1) * GROUP_WIDTH]
        act_scr[:, sl] = _silu(_dwconv3(xin, cw_ref[:, sl], cb_ref[:, sl], row, seq))

    ti = lax.broadcasted_iota(jnp.int32, (q, q), 0)
    si = lax.broadcasted_iota(jnp.int32, (q, q), 1)
    lower = si <= ti
    tri_l = jnp.where(lower, 1.0, 0.0).astype(F32)
    tri_u = jnp.where(si >= ti, 1.0, 0.0).astype(F32)
    lane = lax.broadcasted_iota(jnp.int32, (1, LANE), 1)
    first = lane < SSD_HEAD_DIM
    a_row = -jnp.exp(alog_row_ref[...])
    a_col = -jnp.exp(alog_col_ref[...])

    def per_head(col, h0):
        return jnp.where(first, col[:, h0:h0 + 1], col[:, h0 + 1:h0 + 2])

    def one_chunk(direction, c, carry, y_scr):
        hb = direction * SSD_HEADS
        tri_col, tri_row = (tri_l, tri_u) if direction == 0 else (tri_u, tri_l)
        keep = lower if direction == 0 else (si >= ti)
        edge = q - 1 if direction == 0 else 0
        rows = pl.ds(pl.multiple_of(c * q, q), q)
        dt_col = _softplus(misc_ref[rows, :] + dtb_row_ref[...])
        dt_row = _softplus(dtr_ref[c] + dtb_col_ref[...])
        cs_col = _dot_hi(tri_col, dt_col * a_row)
        cs_row = _dot_hi(dt_row * a_col, tri_row)
        new = []
        for g in range(SSD_GROUPS):
            gs = slice(g * LANE, (g + 1) * LANE)
            h0 = hb + g * hpg
            xg = act_scr[rows, gs]
            bg = act_scr[rows, GROUP_WIDTH + g * LANE:GROUP_WIDTH + (g + 1) * LANE]
            cg = act_scr[rows, 2 * GROUP_WIDTH + g * LANE:2 * GROUP_WIDTH + (g + 1) * LANE]
            xdt = xg * per_head(dt_col, h0)
            cb = _dot_nt(cg.astype(BF16), bg.astype(BF16))
            y = None
            for r in range(hpg):
                seg = cs_col[:, h0 + r:h0 + r + 1] - cs_row[h0 + r:h0 + r + 1, :]
                lmat = jnp.exp(jnp.where(keep, seg, NEG_INF))
                xh = jnp.where(first if r == 0 else ~first, xdt, 0.0)
                part = _dot((cb * lmat).astype(BF16), xh.astype(BF16))
                y = part if y is None else y + part
            cs_g = per_head(cs_col, h0)
            total = cs_g[edge:edge + 1, :]
            y_scr[rows, gs] = y + _dot(cg.astype(BF16), carry[g].astype(BF16)) * jnp.exp(cs_g)
            st = _dot(bg.T.astype(BF16), (xdt * jnp.exp(total - cs_g)).astype(BF16))
            new.append(carry[g] * jnp.exp(total) + st)
        return new

    def step(i, carry):
        fwd = one_chunk(0, i, carry[:SSD_GROUPS], yf_scr)
        bwd = one_chunk(1, nch - 1 - i, carry[SSD_GROUPS:], yb_scr)
        return tuple(fwd + bwd)

    init = tuple(jnp.zeros((SSD_STATE, LANE), F32) for _ in range(2 * SSD_GROUPS))
    lax.fori_loop(0, nch, step, init)

    dsum = dskip_ref[0:1, :] + dskip_ref[1:2, :]
    y = yf_scr[...] + yb_scr[...] + dsum * act_scr[:, 0:GROUP_WIDTH]
    o_ref[...] = y * _silu(x_ref[:, 0:GROUP_WIDTH])


def _ssd(ssd_in, misc, dt_rows, cw, cb, dtb_row, dtb_col, alog_row, alog_col, dskip):
    bsz, seq, _ = ssd_in.shape
    nch = seq // SSD_CHUNK
    small = (cw, cb, dtb_row, dtb_col, alog_row, alog_col, dskip)
    return pl.pallas_call(
        _ssd_kernel,
        out_shape=jax.ShapeDtypeStruct((bsz, seq, GROUP_WIDTH), F32),
        grid=(bsz,),
        in_specs=[pl.BlockSpec((None, seq, SSD_IN_W), lambda b: (b, 0, 0)),
                  pl.BlockSpec((None, seq, MISC_W), lambda b: (b, 0, 0)),
                  pl.BlockSpec((None, nch, 2 * SSD_HEADS, SSD_CHUNK), lambda b: (b, 0, 0, 0))]
                 + [_resident(a.shape) for a in small],
        out_specs=pl.BlockSpec((None, seq, GROUP_WIDTH), lambda b: (b, 0, 0)),
        scratch_shapes=[pltpu.VMEM((seq, 3 * GROUP_WIDTH), F32)] + [pltpu.VMEM((seq, GROUP_WIDTH), F32)] * 2,
        compiler_params=_cparams(("parallel",)), name="ssd")(ssd_in, misc, dt_rows, *small)


def _out_proj_kernel(ya_ref, yb_ref, yc_ref, yd_ref, h_ref, g_ref, w_ref, lg_ref, lb_ref, o_ref):
    parts = []
    for k, y in enumerate((ya_ref[...], yb_ref[...].T, yc_ref[...], yd_ref[...])):
        parts.append(_rms(y, g_ref[:, k * GROUP_WIDTH:(k + 1) * GROUP_WIDTH]).astype(BF16))
    y = jnp.concatenate(parts, axis=1)
    o_ref[...] = _layer_norm(ALPHA * h_ref[...] + _dot(y, w_ref[...]), lg_ref[...], lb_ref[...])


def _out_proj(ya, ybt, yc, yd, h, g, w, lg, lb):
    n = h.shape[0]
    row = lambda wd: pl.BlockSpec((ROW_TILE, wd), lambda i: (i, 0))
    col = pl.BlockSpec((GROUP_WIDTH, ROW_TILE), lambda i: (0, i))
    return pl.pallas_call(
        _out_proj_kernel,
        out_shape=jax.ShapeDtypeStruct((n, D_MODEL), F32),
        grid=(n // ROW_TILE,),
        in_specs=[row(GROUP_WIDTH), col, row(GROUP_WIDTH), row(GROUP_WIDTH), row(D_MODEL), _resident(g.shape),
                  _resident(w.shape), _resident(lg.shape), _resident(lb.shape)],
        out_specs=row(D_MODEL),
        compiler_params=_cparams(("parallel",)), name="out_proj")(ya, ybt, yc, yd, h, g, w, lg, lb)


def _ffn_kernel(tiles_per_seq, hp_ref, h_ref, hn_ref, p_ref, wg_ref, wu_ref, cw_ref, cb_ref, wd_ref,
                lg_ref, lb_ref, pwp_ref, pwg_ref, pbg_ref, plg_ref, plb_ref, o_ref):
    i = pl.program_id(0)
    ext = ROW_TILE + 2 * FFN_HALO
    h = h_ref[...]
    hb = h.astype(BF16)
    hext = jnp.concatenate([hp_ref[...].astype(BF16), hb, hn_ref[...].astype(BF16)], axis=0)
    row = lax.broadcasted_iota(jnp.int32, (ROW_TILE, 1), 0)
    keep_prev = jnp.where((row == 0) & (i % tiles_per_seq == 0), 0.0, 1.0)
    keep_next = jnp.where((row == ROW_TILE - 1) & (i % tiles_per_seq == tiles_per_seq - 1), 0.0, 1.0)
    f = None
    for c0 in range(0, D_FF, FFN_CHUNK):
        sl = slice(c0, c0 + FFN_CHUNK)
        gate = _dot(hext, wg_ref[:, sl])
        g_prev = pltpu.roll(gate, 1, 0)[FFN_HALO:FFN_HALO + ROW_TILE]
        g_next = pltpu.roll(gate, ext - 1, 0)[FFN_HALO:FFN_HALO + ROW_TILE]
        g_cur = gate[FFN_HALO:FFN_HALO + ROW_TILE]
        conv = (g_prev * keep_prev * cw_ref[0:1, sl] + g_cur * cw_ref[1:2, sl]
                + g_next * keep_next * cw_ref[2:3, sl] + cb_ref[:, sl])
        act = (_silu(conv) * _dot(hb, wu_ref[:, sl])).astype(BF16)
        part = _dot(act, wd_ref[sl, :])
        f = part if f is None else f + part
    h = _layer_norm(ALPHA * h + f, lg_ref[...], lb_ref[...])
    gate = _dot(h.astype(BF16), pwg_ref[...]) + pbg_ref[...]
    e = _dot(p_ref[...].astype(BF16), pwp_ref[...]) / (1.0 + jnp.exp(-gate))
    o_ref[...] = _layer_norm(ALPHA * h + e, plg_ref[...], plb_ref[...])


def _ffn_ple(h, p, seq, wg, wu, cw, cb, wd, lg, lb, pwp, pwg, pbg, plg, plb):
    n = h.shape[0]
    per_tile = ROW_TILE // FFN_HALO
    last = n // FFN_HALO - 1
    small = (wg, wu, cw, cb, wd, lg, lb, pwp, pwg, pbg, plg, plb)
    return pl.pallas_call(
        functools.partial(_ffn_kernel, seq // ROW_TILE),
        out_shape=jax.ShapeDtypeStruct((n, D_MODEL), F32),
        grid=(n // ROW_TILE,),
        in_specs=[pl.BlockSpec((FFN_HALO, D_MODEL), lambda i: (jnp.maximum(i * per_tile - 1, 0), 0)),
                  pl.BlockSpec((ROW_TILE, D_MODEL), lambda i: (i, 0)),
                  pl.BlockSpec((FFN_HALO, D_MODEL), lambda i: (jnp.minimum((i + 1) * per_tile, last), 0)),
                  pl.BlockSpec((ROW_TILE, PLE_DIM), lambda i: (i, 0))]
                 + [_resident(a.shape) for a in small],
        out_specs=pl.BlockSpec((ROW_TILE, D_MODEL), lambda i: (i, 0)),
        compiler_params=_cparams(("parallel",)), name="ffn_ple")(h, h, h, p, *small)


def _pad_cols(w, width):
    return jnp.pad(w, ((0, 0), (0, width - w.shape[1])))


def _in_proj_weight(w):
    o = np.cumsum([0, 256, 128, 32, 768, 256, 128, 128, 256, 768, 8])
    cq, ckv, kr, hy, sq, sk, sv, z, xbc, dt = (w[:, o[i]:o[i + 1]] for i in range(10))
    zc = lambda n: jnp.zeros((w.shape[0], n), w.dtype)
    misc = [dt, zc(KR_LANE - dt.shape[1]), kr, zc(MISC_W - KR_LANE - MLA_ROPE)]
    return jnp.concatenate([cq, ckv] + misc + [hy, sq, sk, sv, z, xbc], 1).astype(BF16)


def _mla_weights(w_uq, w_ukv):
    r = w_uq.shape[0]
    half = MLA_ROPE // 2
    qh = w_uq.reshape(r, MLA_HEADS, MLA_NOPE + MLA_ROPE)
    rope = qh[..., MLA_NOPE:]
    rope_sw = jnp.concatenate([rope[..., half:], rope[..., :half]], -1)
    z = lambda n: jnp.zeros((r, MLA_HEADS, n), w_uq.dtype)
    wq = jnp.concatenate([qh, z(LANE - MLA_NOPE - MLA_ROPE)], -1).reshape(r, MLA_HEADS * LANE)
    wqs = jnp.concatenate([z(MLA_NOPE), rope_sw, z(LANE - MLA_NOPE - MLA_ROPE)], -1).reshape(r, MLA_HEADS * LANE)
    rk = w_ukv.shape[0]
    kvh = w_ukv.reshape(rk, MLA_HEADS, MLA_NOPE + MLA_V)
    zk = jnp.zeros((rk, MLA_HEADS, LANE - MLA_NOPE), w_ukv.dtype)
    wk = jnp.concatenate([kvh[..., :MLA_NOPE], zk], -1).reshape(rk, MLA_HEADS * LANE)
    v = kvh[..., MLA_NOPE:]
    zv = jnp.zeros_like(v)
    odd = (jnp.arange(MLA_HEADS) % 2 == 1)[None, :, None]
    wv = jnp.concatenate([jnp.where(odd, zv, v), jnp.where(odd, v, zv)], -1).reshape(rk, MLA_HEADS * LANE)
    return wq.astype(BF16), wqs.astype(BF16), jnp.concatenate([wk, wv], 1).astype(BF16)


def _rope_tables(seq):
    inv_freq = ROPE_THETA ** (-jnp.arange(0, MLA_ROPE, 2, dtype=F32) / MLA_ROPE)
    ang = jnp.arange(seq, dtype=F32)[:, None] * inv_freq[None, :]
    cos, sin = jnp.cos(ang), jnp.sin(ang)
    one = jnp.ones((seq, MLA_NOPE), F32)
    zero = lambda n: jnp.zeros((seq, n), F32)
    ct = jnp.concatenate([one, cos, cos, zero(LANE - MLA_NOPE - MLA_ROPE)], 1)
    st = jnp.concatenate([zero(MLA_NOPE), -sin, sin, zero(LANE - MLA_NOPE - MLA_ROPE)], 1)
    return ct, st


def _hy_tables(seq):
    t = jnp.linspace(0.0, 1.0, seq, dtype=F32)[:, None]
    ang = 2.0 * math.pi * jnp.arange(seq, dtype=F32)[:, None] / seq
    bands = jnp.linspace(1e-4, HY_BANDS - 1, HY_BANDS, dtype=F32)[None, :]
    feat = jnp.concatenate([t, jnp.cos(bands * ang), -jnp.sin(bands * ang)], -1)
    max_decay = math.log(HY_TARGET) / HY_FAST_DECAY
    min_decay = math.log(HY_TARGET) / HY_SLOW_DECAY
    deltas = jnp.linspace(min_decay, max_decay, HY_WIDTH, dtype=F32)
    decay = jnp.exp(-t * jnp.abs(deltas)[None, :])
    return _pad_cols(feat, LANE), decay


def _pad2(w, rows, cols):
    return jnp.pad(w, ((0, rows - w.shape[0]), (0, cols - w.shape[1])))


def _hy_lag_filters(filt, seq):
    hf = filt.reshape(seq, HY_ORDER, 2, HY_WIDTH)
    fwd, bwd = hf[:, :, 0], hf[:, :, 1]
    k2 = jnp.concatenate([jnp.zeros_like(bwd[:1]), bwd[:0:-1], fwd], 0)
    return jnp.transpose(k2, (2, 1, 0))


def kernel(x, p, emb_ln_g, emb_ln_b, w_in, mla_q_norm, mla_kv_norm, mla_w_uq, mla_w_ukv, hy_conv_w, hy_conv_b, hy_f_w1, hy_f_b1, hy_f_freq, hy_f_w2, hy_f_b2, hy_f_w3, hy_bias, swa_sink, ssd_conv_w, ssd_conv_b, ssd_dt_bias, ssd_a_log, ssd_d, mix_norm_g, w_out, ln1_g, ln1_b, ffn_w_gate, ffn_w_up, ffn_conv_w, ffn_conv_b, ffn_w_down, ln2_g, ln2_b, ple_w_proj, ple_w_gate, ple_b_gate, ln3_g, ln3_b):
    bsz, seq, d = x.shape
    n = bsz * seq
    row = lambda v: v.reshape(1, -1)
    ct, st = _rope_tables(seq)
    feat, decay = _hy_tables(seq)
    h = x.reshape(n, d)
    for i in range(DEPTH):
        w_in_i = _in_proj_weight(w_in[i])
        if i == 0:
            h, mla_in, misc, hy_u, swa_in, ssd_in = _in_proj(h, w_in_i, (row(emb_ln_g), row(emb_ln_b)))
        else:
            mla_in, misc, hy_u, swa_in, ssd_in = _in_proj(h, w_in_i)
        misc = misc.reshape(bsz, seq, MISC_W)

        wq, wqs, wkv = _mla_weights(mla_w_uq[i], mla_w_ukv[i])
        y_a = _mla(mla_in.reshape(bsz, seq, MLA_IN_W), misc, ct, st, row(mla_q_norm[i]), row(mla_kv_norm[i]),
                   wq, wqs, wkv)

        filt = _hy_filter(feat, _pad2(hy_f_w1[i], LANE, LANE), _pad_cols(row(hy_f_b1[i]), LANE),
                          _pad_cols(row(hy_f_freq[i]), LANE), _pad2(hy_f_w2[i], LANE, LANE),
                          _pad_cols(row(hy_f_b2[i]), LANE), _pad2(hy_f_w3[i], LANE, 2 * HY_ORDER * HY_WIDTH),
                          decay)
        uct = _hy_pre(hy_u.reshape(bsz, seq, HY_IN_W), hy_conv_w[i], row(hy_conv_b[i]))
        y_bt = _hy_conv(uct.reshape(HY_IN_W, bsz, seq), _hy_lag_filters(filt, seq), hy_bias[i])

        y_c = _swa(swa_in.reshape(bsz, seq, SWA_IN_W), swa_sink[i])

        dt_rows = jnp.transpose(
            misc[:, :, 0:2 * SSD_HEADS].reshape(bsz, seq // SSD_CHUNK, SSD_CHUNK, 2 * SSD_HEADS), (0, 1, 3, 2))
        dtb = ssd_dt_bias[i].reshape(-1)
        alog = ssd_a_log[i].reshape(-1)
        dskip = jnp.repeat(ssd_d[i], SSD_HEAD_DIM, axis=1)
        y_d = _ssd(ssd_in.reshape(bsz, seq, SSD_IN_W), misc, dt_rows, ssd_conv_w[i], row(ssd_conv_b[i]),
                   _pad_cols(row(dtb), LANE), dtb.reshape(-1, 1), _pad_cols(row(alog), LANE), alog.reshape(-1, 1),
                   dskip)

        flat = lambda y: y.reshape(n, GROUP_WIDTH)
        h = _out_proj(flat(y_a), y_bt.reshape(HY_WIDTH, n), flat(y_c), flat(y_d), h, row(mix_norm_g[i]),
                      w_out[i].astype(BF16), row(ln1_g[i]), row(ln1_b[i]))
        h = _ffn_ple(h, p[i].reshape(n, PLE_DIM), seq, ffn_w_gate[i].astype(BF16), ffn_w_up[i].astype(BF16),
                     ffn_conv_w[i], row(ffn_conv_b[i]), ffn_w_down[i].astype(BF16), row(ln2_g[i]), row(ln2_b[i]),
                     ple_w_proj[i].astype(BF16), ple_w_gate[i].astype(BF16), row(ple_b_gate[i]),
                     row(ln3_g[i]), row(ln3_b[i]))
    return h.reshape(bsz, seq, d)
```

```python
import functools
import math

import jax
import jax.numpy as jnp
import numpy as np
from jax import lax
from jax.experimental import pallas as pl
from jax.experimental.pallas import tpu as pltpu

F32 = jnp.float32
BF16 = jnp.bfloat16

D_MODEL = 1024
DEPTH = 2
PLE_DIM = 256
GROUP_WIDTH = 256
BLOCK = 128

MLA_HEADS = 4
MLA_Q_RANK = 256
MLA_KV_RANK = 128
MLA_NOPE = 64
MLA_ROPE = 32
MLA_V = 64
ROPE_THETA = 10000.0

HY_WIDTH = 256
HY_ORDER = 2
HY_EMB = 33
HY_BANDS = 16
HY_FAST_DECAY = 0.3
HY_SLOW_DECAY = 1.5
HY_TARGET = 1e-2
HY_BLK = 256

SWA_HEADS = 4
SWA_KV_HEADS = 2
SWA_HEAD_DIM = 64
SWA_WINDOW = 128

SSD_D_INNER = 256
SSD_HEAD_DIM = 64
SSD_HEADS = 4
SSD_GROUPS = 2
SSD_STATE = 128
SSD_CHUNK = 128

D_FF = 2816
LN_EPS = 1e-5
RMS_EPS = 1e-6
NEG_INF = -1e30
ALPHA = (2.0 * DEPTH) ** 0.25

LANE = 128
VMEM_LIMIT = 56 * 1024 * 1024

MLA_IN_W = 384
MISC_W = 128
HY_IN_W = 768
SWA_IN_W = 512
SSD_IN_W = 1024
IN_WIDTHS = (MLA_IN_W, MISC_W, HY_IN_W, SWA_IN_W, SSD_IN_W)
KR_LANE = MLA_NOPE

ROW_TILE = 512
FFN_HALO = 16
FFN_CHUNK = 1408


def _cparams(sem):
    return pltpu.CompilerParams(dimension_semantics=sem, vmem_limit_bytes=VMEM_LIMIT)


def _resident(shape):
    nd = len(shape)
    return pl.BlockSpec(shape, lambda *_: (0,) * nd, pipeline_mode=pl.Buffered(1))


def _layer_norm(x, g, b):
    mu = jnp.mean(x, -1, keepdims=True)
    xc = x - mu
    var = jnp.mean(xc * xc, -1, keepdims=True)
    return xc * lax.rsqrt(var + LN_EPS) * g + b


def _rms(x, g):
    return x * lax.rsqrt(jnp.mean(x * x, -1, keepdims=True) + RMS_EPS) * g


def _silu(x):
    return x / (1.0 + jnp.exp(-x))


def _dot(a, b):
    return jnp.dot(a, b, preferred_element_type=F32)


def _dot_nt(a, b):
    return lax.dot_general(a, b, (((1,), (1,)), ((), ())), preferred_element_type=F32)


def _dot_hi(a, b):
    return jnp.dot(a, b, preferred_element_type=F32, precision=lax.Precision.HIGHEST)


def _in_proj_kernel(apply_ln, *refs):
    if apply_ln:
        x_ref, g_ref, b_ref, w_ref, h_ref = refs[:5]
        out_refs = refs[5:]
        h = _layer_norm(x_ref[...], g_ref[...], b_ref[...])
        h_ref[...] = h
    else:
        x_ref, w_ref = refs[:2]
        out_refs = refs[2:]
        h = x_ref[...]
    u = _dot(h.astype(BF16), w_ref[...])
    o = 0
    for ref, w in zip(out_refs, IN_WIDTHS):
        ref[...] = u[:, o:o + w]
        o += w
    out_refs[-1][...] = u[:, MLA_IN_W:MLA_IN_W + MISC_W].T[0:2 * SSD_HEADS, :]


def _in_proj(x, w, ln=None):
    n = x.shape[0]
    row = lambda wd: pl.BlockSpec((ROW_TILE, wd), lambda i: (i, 0))
    outs = [jax.ShapeDtypeStruct((n, wd), F32) for wd in IN_WIDTHS]
    outs.append(jax.ShapeDtypeStruct((2 * SSD_HEADS, n), F32))
    out_specs = [row(wd) for wd in IN_WIDTHS] + [pl.BlockSpec((2 * SSD_HEADS, ROW_TILE), lambda i: (0, i))]
    if ln is not None:
        args = (x, ln[0], ln[1], w)
        in_specs = [row(D_MODEL), _resident((1, D_MODEL)), _resident((1, D_MODEL)), _resident(w.shape)]
        outs = [jax.ShapeDtypeStruct((n, D_MODEL), F32)] + outs
        out_specs = [row(D_MODEL)] + out_specs
    else:
        args = (x, w)
        in_specs = [row(D_MODEL), _resident(w.shape)]
    return pl.pallas_call(
        functools.partial(_in_proj_kernel, ln is not None),
        out_shape=outs, grid=(n // ROW_TILE,), in_specs=in_specs, out_specs=out_specs,
        compiler_params=_cparams(("parallel",)), name="in_proj")(*args)


MLA_TQ = 256


def _mla_kernel(x_ref, misc_ref, ct_ref, st_ref, gq_ref, gkv_ref, wq_ref, wkv_ref, o_ref,
                q_scr, k_scr, v_scr):
    seq = x_ref.shape[0]
    scale = (MLA_NOPE + MLA_ROPE) ** -0.5 * math.log2(math.e)
    half = MLA_ROPE // 2
    ct = ct_ref[...]
    st = st_ref[...]
    cqn = _rms(x_ref[:, 0:256], gq_ref[...]).astype(BF16)
    ckvn = _rms(x_ref[:, 256:384], gkv_ref[...]).astype(BF16)
    lane = lax.broadcasted_iota(jnp.int32, (1, LANE), 1)
    on_rope = (lane >= KR_LANE) & (lane < KR_LANE + MLA_ROPE)

    def rotate(x):
        return x * ct + pltpu.roll(x, LANE - half, 1) * st

    k_rope = jnp.where(on_rope, rotate(misc_ref[...]), 0.0)
    for h in range(MLA_HEADS):
        sl = slice(h * LANE, (h + 1) * LANE)
        q_scr[:, sl] = (rotate(_dot(cqn, wq_ref[:, sl])) * scale).astype(BF16)
        k_scr[:, sl] = (_dot(ckvn, wkv_ref[:, sl]) + k_rope).astype(BF16)
        vsl = slice(MLA_HEADS * LANE + h * LANE, MLA_HEADS * LANE + (h + 1) * LANE)
        v_scr[:, sl] = _dot(ckvn, wkv_ref[:, vsl]).astype(BF16)

    def q_tile(t, carry):
        rows = pl.ds(pl.multiple_of(t * MLA_TQ, MLA_TQ), MLA_TQ)
        for pair in range(MLA_HEADS // 2):
            acc = None
            for h in (2 * pair, 2 * pair + 1):
                sl = slice(h * LANE, (h + 1) * LANE)
                s = _dot_nt(q_scr[rows, sl], k_scr[:, sl])
                m = jnp.max(s, -1, keepdims=True)
                e = jnp.exp2(s - m)
                l = jnp.sum(e, -1, keepdims=True)
                o = _dot(e.astype(BF16), v_scr[:, sl]) / l
                acc = o if acc is None else acc + o
            o_ref[rows, pair * LANE:(pair + 1) * LANE] = acc
        return carry

    lax.fori_loop(0, seq // MLA_TQ, q_tile, 0)


def _mla(mla_in, misc, ct, st, gq, gkv, wq, wkv):
    bsz, seq, _ = mla_in.shape
    return pl.pallas_call(
        _mla_kernel,
        out_shape=jax.ShapeDtypeStruct((bsz, seq, GROUP_WIDTH), F32),
        grid=(bsz,),
        in_specs=[pl.BlockSpec((None, seq, MLA_IN_W), lambda b: (b, 0, 0)),
                  pl.BlockSpec((None, seq, MISC_W), lambda b: (b, 0, 0)),
                  _resident(ct.shape), _resident(st.shape), _resident(gq.shape), _resident(gkv.shape),
                  _resident(wq.shape), _resident(wkv.shape)],
        out_specs=pl.BlockSpec((None, seq, GROUP_WIDTH), lambda b: (b, 0, 0)),
        scratch_shapes=[pltpu.VMEM((seq, MLA_HEADS * LANE), BF16)] * 3,
        compiler_params=_cparams(("parallel",)), name="mla")(mla_in, misc, ct, st, gq, gkv, wq, wkv)


def _hy_filter_kernel(feat_ref, w1_ref, b1_ref, fr_ref, w2_ref, b2_ref, w3_ref, decay_ref, o_ref):
    fr = fr_ref[...]
    h = jnp.sin(fr * (_dot_hi(feat_ref[...], w1_ref[...]) + b1_ref[...]))
    h = jnp.sin(fr * (_dot_hi(h, w2_ref[...]) + b2_ref[...]))
    decay = decay_ref[...]
    for k in range(2 * HY_ORDER):
        sl = slice(k * HY_WIDTH, (k + 1) * HY_WIDTH)
        o_ref[:, sl] = _dot_hi(h, w3_ref[:, sl]) * decay


def _hy_filter(feat, w1, b1, fr, w2, b2, w3, decay):
    seq = feat.shape[0]
    return pl.pallas_call(
        _hy_filter_kernel,
        out_shape=jax.ShapeDtypeStruct((seq, 2 * HY_ORDER * HY_WIDTH), F32),
        compiler_params=pltpu.CompilerParams(vmem_limit_bytes=VMEM_LIMIT),
        name="hy_filter")(feat, w1, b1, fr, w2, b2, w3, decay)


def _shift_rows(x, row, seq):
    prev = jnp.where(row == 0, 0.0, pltpu.roll(x, 1, 0))
    nxt = jnp.where(row == seq - 1, 0.0, pltpu.roll(x, seq - 1, 0))
    return prev, nxt


def _dwconv3(x, w, b, row, seq):
    prev, nxt = _shift_rows(x, row, seq)
    return prev * w[0:1] + x * w[1:2] + nxt * w[2:3] + b


def _hy_pre_kernel(u_ref, w_ref, b_ref, o_ref):
    seq = u_ref.shape[0]
    row = lax.broadcasted_iota(jnp.int32, (seq, HY_WIDTH), 0)
    for k in range(HY_ORDER + 1):
        sl = slice(k * HY_WIDTH, (k + 1) * HY_WIDTH)
        o_ref[sl, :] = _dwconv3(u_ref[:, sl], w_ref[:, sl], b_ref[:, sl], row, seq).T


def _hy_pre(hy_u, w, b):
    bsz, seq, wd = hy_u.shape
    return pl.pallas_call(
        _hy_pre_kernel,
        out_shape=jax.ShapeDtypeStruct((wd, bsz * seq), F32),
        grid=(bsz,),
        in_specs=[pl.BlockSpec((None, seq, wd), lambda b: (b, 0, 0)), _resident(w.shape), _resident(b.shape)],
        out_specs=pl.BlockSpec((wd, seq), lambda b: (0, b)),
        compiler_params=_cparams(("parallel",)), name="hy_pre")(hy_u, w, b)


HY_ROWS16 = 16
HY_CH = 4


def _hy_conv_kernel(bias_ref, v_ref, x1_ref, x2_ref, f_ref, o_ref, r_scr):
    _, bsz, seq = v_ref.shape
    nblk = seq // HY_BLK
    c0 = pl.program_id(0) * HY_CH

    def build_toeplitz(slot, taps):
        k16 = jnp.broadcast_to(taps, (HY_ROWS16, 2 * seq))
        k16 = pltpu.roll(k16, 0, 1, stride=1, stride_axis=0)
        for g in range(LANE // HY_ROWS16):
            blk = (k16 if g == 0 else pltpu.roll(k16, g * HY_ROWS16, 1)).astype(BF16)
            r_scr[slot, g * HY_ROWS16:(g + 1) * HY_ROWS16, :] = blk
            r_scr[slot, LANE + g * HY_ROWS16:LANE + (g + 1) * HY_ROWS16, LANE:] = blk[:, :2 * seq - LANE]

    def long_conv(z, slot):
        zr = jnp.concatenate([z[:, j * HY_BLK:(j + 1) * HY_BLK] for j in range(nblk)], axis=0).astype(BF16)
        acc = [None] * nblk
        for d in range(-(nblk - 1), nblk):
            lo_in, hi_in = max(0, -d), nblk - max(0, d)
            w_d = r_scr[slot, :, seq + d * HY_BLK: seq + (d + 1) * HY_BLK]
            part = _dot(zr[lo_in * bsz:hi_in * bsz, :], w_d)
            for j in range(lo_in, hi_in):
                piece = part[(j - lo_in) * bsz:(j - lo_in + 1) * bsz, :]
                acc[j + d] = piece if acc[j + d] is None else acc[j + d] + piece
        return jnp.concatenate(acc, axis=1)

    for ch in range(HY_CH):
        for order in range(HY_ORDER):
            build_toeplitz(ch * HY_ORDER + order, f_ref[ch, order:order + 1, :])
    for ch in range(HY_CH):
        z0 = v_ref[ch]
        z1 = x1_ref[ch] * (long_conv(z0, ch * HY_ORDER) + bias_ref[0, c0 + ch] * z0)
        o_ref[ch] = x2_ref[ch] * (long_conv(z1, ch * HY_ORDER + 1) + bias_ref[1, c0 + ch] * z1)


def _hy_conv(uct, filt, bias):
    _, bsz, seq = uct.shape
    per_slab = HY_WIDTH // HY_CH
    chan = lambda k: pl.BlockSpec((HY_CH, bsz, seq), lambda c, k=k: (c + k * per_slab, 0, 0))
    return pl.pallas_call(
        _hy_conv_kernel,
        out_shape=jax.ShapeDtypeStruct((HY_WIDTH, bsz, seq), F32),
        grid=(per_slab,),
        in_specs=[pl.BlockSpec(memory_space=pltpu.SMEM), chan(0), chan(1), chan(2),
                  pl.BlockSpec((HY_CH, HY_ORDER, 2 * seq), lambda c: (c, 0, 0))],
        out_specs=pl.BlockSpec((HY_CH, bsz, seq), lambda c: (c, 0, 0)),
        scratch_shapes=[pltpu.VMEM((HY_CH * HY_ORDER, HY_BLK, 2 * seq), BF16)],
        compiler_params=_cparams(("parallel",)), name="hy_conv")(bias, uct, uct, uct, filt)


def _swa_kernel(sink_ref, x_ref, o_ref, k_scr, vlo_scr, vhi_scr, bias_scr):
    seq = x_ref.shape[0]
    nb = seq // BLOCK
    win = 3 * BLOCK
    grp = SWA_HEADS // SWA_KV_HEADS
    log2e = math.log2(math.e)
    scale = SWA_HEAD_DIM ** -0.5 * log2e
    kvw = SWA_KV_HEADS * LANE
    zeros = jnp.zeros((BLOCK, kvw), BF16)
    lane = lax.broadcasted_iota(jnp.int32, (1, LANE), 1)
    lo_half = lane < SWA_HEAD_DIM
    for scr in (k_scr, vlo_scr, vhi_scr):
        scr[0:BLOCK, :] = zeros
        scr[BLOCK + seq:2 * BLOCK + seq, :] = zeros
    body = slice(BLOCK, BLOCK + seq)
    k_both = x_ref[:, 256:256 + LANE]
    k_flip = pltpu.roll(k_both, SWA_HEAD_DIM, 1)
    k_scr[body, 0:LANE] = jnp.where(lo_half, k_both, k_flip).astype(BF16)
    k_scr[body, LANE:2 * LANE] = jnp.where(lo_half, k_flip, k_both).astype(BF16)
    v_both = x_ref[:, 256 + LANE:256 + 2 * LANE]
    v_flip = pltpu.roll(v_both, SWA_HEAD_DIM, 1)
    vlo_scr[body, 0:LANE] = jnp.where(lo_half, v_both, 0.0).astype(BF16)
    vhi_scr[body, 0:LANE] = jnp.where(lo_half, 0.0, v_flip).astype(BF16)
    vlo_scr[body, LANE:2 * LANE] = jnp.where(lo_half, v_flip, 0.0).astype(BF16)
    vhi_scr[body, LANE:2 * LANE] = jnp.where(lo_half, 0.0, v_both).astype(BF16)

    qi = lax.broadcasted_iota(jnp.int32, (grp * BLOCK, win), 0)
    ji = lax.broadcasted_iota(jnp.int32, (grp * BLOCK, win), 1)
    upper = qi >= BLOCK
    dist = jnp.abs(jnp.where(upper, qi - BLOCK, qi) + BLOCK - ji)
    slopes = [(2.0 ** (-8.0 / SWA_HEADS)) ** (i + 1) * log2e for i in range(SWA_HEADS)]
    for j in range(SWA_KV_HEADS):
        slope = jnp.where(upper, slopes[grp * j + 1], slopes[grp * j])
        bias_scr[j] = jnp.where(dist <= SWA_WINDOW, -slope * dist.astype(F32), NEG_INF)
    col = lax.broadcasted_iota(jnp.int32, (1, win), 1)
    row_up = lax.broadcasted_iota(jnp.int32, (grp * BLOCK, 1), 0) >= BLOCK

    def block(n, edge):
        r0 = n * BLOCK if isinstance(n, int) else pl.multiple_of(n * BLOCK, BLOCK)
        for j in range(SWA_KV_HEADS):
            sl = slice(j * LANE, (j + 1) * LANE)
            qp = x_ref[pl.ds(r0, BLOCK), sl] * scale
            q2 = jnp.concatenate([jnp.where(lo_half, qp, 0.0), jnp.where(lo_half, 0.0, qp)], 0).astype(BF16)
            sc = _dot_nt(q2, k_scr[pl.ds(r0, win), sl]) + bias_scr[j]
            if edge == "first":
                sc = jnp.where(col >= BLOCK, sc, NEG_INF)
            elif edge == "last":
                sc = jnp.where(col < 2 * BLOCK, sc, NEG_INF)
            sk = jnp.where(row_up, sink_ref[grp * j + 1], sink_ref[grp * j]) * log2e
            m = jnp.maximum(jnp.max(sc, -1, keepdims=True), sk)
            e = jnp.exp2(sc - m)
            den = jnp.sum(e, -1, keepdims=True) + jnp.exp2(sk - m)
            eb = e.astype(BF16)
            o_lo = _dot(eb[0:BLOCK], vlo_scr[pl.ds(r0, win), sl]) / den[0:BLOCK]
            o_hi = _dot(eb[BLOCK:], vhi_scr[pl.ds(r0, win), sl]) / den[BLOCK:]
            o_ref[pl.ds(r0, BLOCK), sl] = o_lo + o_hi

    def inner(n, carry):
        block(n, None)
        return carry

    block(0, "first")
    lax.fori_loop(1, nb - 1, inner, 0, unroll=2)
    block(nb - 1, "last")


def _swa(swa_in, sink):
    bsz, seq, _ = swa_in.shape
    kv_scr = pltpu.VMEM((seq + 2 * BLOCK, SWA_KV_HEADS * LANE), BF16)
    grp = SWA_HEADS // SWA_KV_HEADS
    return pl.pallas_call(
        _swa_kernel,
        out_shape=jax.ShapeDtypeStruct((bsz, seq, GROUP_WIDTH), F32),
        grid=(bsz,),
        in_specs=[pl.BlockSpec(memory_space=pltpu.SMEM),
                  pl.BlockSpec((None, seq, SWA_IN_W), lambda b: (b, 0, 0))],
        out_specs=pl.BlockSpec((None, seq, GROUP_WIDTH), lambda b: (b, 0, 0)),
        scratch_shapes=[kv_scr, kv_scr, kv_scr, pltpu.VMEM((SWA_KV_HEADS, grp * BLOCK, 3 * BLOCK), F32)],
        compiler_params=_cparams(("parallel",)), name="swa")(sink, swa_in)


def _softplus(x):
    return jnp.maximum(x, 0.0) + jnp.log1p(jnp.exp(-jnp.abs(x)))


def _split3(x):
    hi = x.astype(BF16)
    rest = x - hi.astype(F32)
    mid = rest.astype(BF16)
    return hi, mid, (rest - mid.astype(F32)).astype(BF16)


def _ssd_kernel(x_ref, dtr_ref, cw_ref, cb_ref, dtb_col_ref, alog_col_ref, dskip_ref, o_ref,
                act_scr, yf_scr, yb_scr, rows_scr, cols_scr):
    seq = x_ref.shape[0]
    q = SSD_CHUNK
    nch = seq // q
    nh = 2 * SSD_HEADS
    hpg = SSD_HEADS // SSD_GROUPS
    row = lax.broadcasted_iota(jnp.int32, (seq, GROUP_WIDTH), 0)
    for k in range(3):
        sl = slice(k * GROUP_WIDTH, (k + 1) * GROUP_WIDTH)
        xin = x_ref[:, 256 + k * GROUP_WIDTH:256 + (k + 1) * GROUP_WIDTH]
        act_scr[:, sl] = _silu(_dwconv3(xin, cw_ref[:, sl], cb_ref[:, sl], row, seq))

    ti = lax.broadcasted_iota(jnp.int32, (q, q), 0)
    si = lax.broadcasted_iota(jnp.int32, (q, q), 1)
    lower = si <= ti
    upper = si >= ti
    tri_l = jnp.where(lower, 1.0, 0.0).astype(BF16)
    tri_u = jnp.where(upper, 1.0, 0.0).astype(BF16)
    lane = lax.broadcasted_iota(jnp.int32, (1, LANE), 1)
    first = lane < SSD_HEAD_DIM

    dt_all = _softplus(dtr_ref[...] + dtb_col_ref[...])
    da_parts = _split3(dt_all * -jnp.exp(alog_col_ref[...]))
    row_tri = jnp.concatenate([tri_u, tri_l], axis=1)
    col_tri = jnp.concatenate([tri_l, tri_u], axis=0)
    rows_scr[...] = sum(_dot(part, row_tri) for part in da_parts)
    cols_scr[0:2 * q, :] = sum(_dot_nt(col_tri, part) for part in da_parts)
    cols_scr[2 * q:3 * q, :] = dt_all.T

    def per_head(col, h0):
        return jnp.where(first, col[:, h0:h0 + 1], col[:, h0 + 1:h0 + 2])

    def one_chunk(direction, c, carry, y_scr):
        hb = direction * SSD_HEADS
        keep = lower if direction == 0 else upper
        edge = q - 1 if direction == 0 else 0
        rows = slice(c * q, (c + 1) * q)
        heads = slice(c * nh, (c + 1) * nh)
        cs_col = cols_scr[direction * q:(direction + 1) * q, heads]
        cs_row = rows_scr[heads, direction * q:(direction + 1) * q]
        dt_col = cols_scr[2 * q:3 * q, heads]
        new = []
        for g in range(SSD_GROUPS):
            gs = slice(g * LANE, (g + 1) * LANE)
            h0 = hb + g * hpg
            xg = act_scr[rows, gs]
            bg = act_scr[rows, GROUP_WIDTH + g * LANE:GROUP_WIDTH + (g + 1) * LANE]
            cg = act_scr[rows, 2 * GROUP_WIDTH + g * LANE:2 * GROUP_WIDTH + (g + 1) * LANE]
            xdt = xg * per_head(dt_col, h0)
            cb = _dot_nt(cg.astype(BF16), bg.astype(BF16))
            y = None
            for r in range(hpg):
                seg = cs_col[:, h0 + r:h0 + r + 1] - cs_row[h0 + r:h0 + r + 1, :]
                lmat = jnp.exp(jnp.where(keep, seg, NEG_INF))
                xh = jnp.where(first if r == 0 else ~first, xdt, 0.0)
                part = _dot((cb * lmat).astype(BF16), xh.astype(BF16))
                y = part if y is None else y + part
            cs_g = per_head(cs_col, h0)
            total = cs_g[edge:edge + 1, :]
            y_scr[rows, gs] = y + _dot(cg.astype(BF16), carry[g].astype(BF16)) * jnp.exp(cs_g)
            st = _dot(bg.T.astype(BF16), (xdt * jnp.exp(total - cs_g)).astype(BF16))
            new.append(carry[g] * jnp.exp(total) + st)
        return new

    fwd = [jnp.zeros((SSD_STATE, LANE), F32) for _ in range(SSD_GROUPS)]
    bwd = [jnp.zeros((SSD_STATE, LANE), F32) for _ in range(SSD_GROUPS)]
    for i in range(nch):
        fwd = one_chunk(0, i, fwd, yf_scr)
        bwd = one_chunk(1, nch - 1 - i, bwd, yb_scr)

    dsum = dskip_ref[0:1, :] + dskip_ref[1:2, :]
    y = yf_scr[...] + yb_scr[...] + dsum * act_scr[:, 0:GROUP_WIDTH]
    o_ref[...] = y * _silu(x_ref[:, 0:GROUP_WIDTH])


def _ssd(ssd_in, dt_rows, cw, cb, dtb_col, alog_col, dskip):
    bsz, seq, _ = ssd_in.shape
    q = SSD_CHUNK
    nch = seq // q
    nh = 2 * SSD_HEADS
    small = (cw, cb, dtb_col, alog_col, dskip)
    return pl.pallas_call(
        _ssd_kernel,
        out_shape=jax.ShapeDtypeStruct((bsz, seq, GROUP_WIDTH), F32),
        grid=(bsz,),
        in_specs=[pl.BlockSpec((None, seq, SSD_IN_W), lambda b: (b, 0, 0)),
                  pl.BlockSpec((None, nch * nh, q), lambda b: (b, 0, 0))]
                 + [_resident(a.shape) for a in small],
        out_specs=pl.BlockSpec((None, seq, GROUP_WIDTH), lambda b: (b, 0, 0)),
        scratch_shapes=[pltpu.VMEM((seq, 3 * GROUP_WIDTH), F32)] + [pltpu.VMEM((seq, GROUP_WIDTH), F32)] * 2
                       + [pltpu.VMEM((nch * nh, 2 * q), F32), pltpu.VMEM((3 * q, nch * nh), F32)],
        compiler_params=_cparams(("parallel",)), name="ssd")(ssd_in, dt_rows, *small)


def _out_proj_kernel(ya_ref, yb_ref, yc_ref, yd_ref, h_ref, g_ref, w_ref, lg_ref, lb_ref, o_ref):
    parts = []
    for k, y in enumerate((ya_ref[...], yb_ref[...].T, yc_ref[...], yd_ref[...])):
        parts.append(_rms(y, g_ref[:, k * GROUP_WIDTH:(k + 1) * GROUP_WIDTH]).astype(BF16))
    y = jnp.concatenate(parts, axis=1)
    o_ref[...] = _layer_norm(ALPHA * h_ref[...] + _dot(y, w_ref[...]), lg_ref[...], lb_ref[...])


def _out_proj(ya, ybt, yc, yd, h, g, w, lg, lb):
    n = h.shape[0]
    row = lambda wd: pl.BlockSpec((ROW_TILE, wd), lambda i: (i, 0))
    col = pl.BlockSpec((GROUP_WIDTH, ROW_TILE), lambda i: (0, i))
    return pl.pallas_call(
        _out_proj_kernel,
        out_shape=jax.ShapeDtypeStruct((n, D_MODEL), F32),
        grid=(n // ROW_TILE,),
        in_specs=[row(GROUP_WIDTH), col, row(GROUP_WIDTH), row(GROUP_WIDTH), row(D_MODEL), _resident(g.shape),
                  _resident(w.shape), _resident(lg.shape), _resident(lb.shape)],
        out_specs=row(D_MODEL),
        compiler_params=_cparams(("parallel",)), name="out_proj")(ya, ybt, yc, yd, h, g, w, lg, lb)


def _ffn_kernel(tiles_per_seq, hp_ref, h_ref, hn_ref, p_ref, wg_ref, wu_ref, cw_ref, cb_ref, wd_ref,
                lg_ref, lb_ref, pwp_ref, pwg_ref, pbg_ref, plg_ref, plb_ref, o_ref):
    i = pl.program_id(0)
    ext = ROW_TILE + 2 * FFN_HALO
    h = h_ref[...]
    hb = h.astype(BF16)
    hext = jnp.concatenate([hp_ref[...].astype(BF16), hb, hn_ref[...].astype(BF16)], axis=0)
    row = lax.broadcasted_iota(jnp.int32, (ROW_TILE, 1), 0)
    keep_prev = jnp.where((row == 0) & (i % tiles_per_seq == 0), 0.0, 1.0)
    keep_next = jnp.where((row == ROW_TILE - 1) & (i % tiles_per_seq == tiles_per_seq - 1), 0.0, 1.0)
    f = None
    for c0 in range(0, D_FF, FFN_CHUNK):
        sl = slice(c0, c0 + FFN_CHUNK)
        gate = _dot(hext, wg_ref[:, sl])
        g_prev = pltpu.roll(gate, 1, 0)[FFN_HALO:FFN_HALO + ROW_TILE]
        g_next = pltpu.roll(gate, ext - 1, 0)[FFN_HALO:FFN_HALO + ROW_TILE]
        g_cur = gate[FFN_HALO:FFN_HALO + ROW_TILE]
        conv = (g_prev * keep_prev * cw_ref[0:1, sl] + g_cur * cw_ref[1:2, sl]
                + g_next * keep_next * cw_ref[2:3, sl] + cb_ref[:, sl])
        act = (_silu(conv) * _dot(hb, wu_ref[:, sl])).astype(BF16)
        part = _dot(act, wd_ref[sl, :])
        f = part if f is None else f + part
    h = _layer_norm(ALPHA * h + f, lg_ref[...], lb_ref[...])
    gate = _dot(h.astype(BF16), pwg_ref[...]) + pbg_ref[...]
    e = _dot(p_ref[...].astype(BF16), pwp_ref[...]) / (1.0 + jnp.exp(-gate))
    o_ref[...] = _layer_norm(ALPHA * h + e, plg_ref[...], plb_ref[...])


def _ffn_ple(h, p, seq, wg, wu, cw, cb, wd, lg, lb, pwp, pwg, pbg, plg, plb):
    n = h.shape[0]
    per_tile = ROW_TILE // FFN_HALO
    last = n // FFN_HALO - 1
    small = (wg, wu, cw, cb, wd, lg, lb, pwp, pwg, pbg, plg, plb)
    return pl.pallas_call(
        functools.partial(_ffn_kernel, seq // ROW_TILE),
        out_shape=jax.ShapeDtypeStruct((n, D_MODEL), F32),
        grid=(n // ROW_TILE,),
        in_specs=[pl.BlockSpec((FFN_HALO, D_MODEL), lambda i: (jnp.maximum(i * per_tile - 1, 0), 0)),
                  pl.BlockSpec((ROW_TILE, D_MODEL), lambda i: (i, 0)),
                  pl.BlockSpec((FFN_HALO, D_MODEL), lambda i: (jnp.minimum((i + 1) * per_tile, last), 0)),
                  pl.BlockSpec((ROW_TILE, PLE_DIM), lambda i: (i, 0))]
                 + [_resident(a.shape) for a in small],
        out_specs=pl.BlockSpec((ROW_TILE, D_MODEL), lambda i: (i, 0)),
        compiler_params=_cparams(("parallel",)), name="ffn_ple")(h, h, h, p, *small)


def _pad_cols(w, width):
    return jnp.pad(w, ((0, 0), (0, width - w.shape[1])))


def _in_proj_weight(w):
    o = np.cumsum([0, 256, 128, 32, 768, 256, 128, 128, 256, 768, 8])
    cq, ckv, kr, hy, sq, sk, sv, z, xbc, dt = (w[:, o[i]:o[i + 1]] for i in range(10))
    zc = lambda n: jnp.zeros((w.shape[0], n), w.dtype)
    half = MLA_ROPE // 2
    misc = [dt, zc(KR_LANE - dt.shape[1]), kr, kr[:, :half], zc(MISC_W - KR_LANE - MLA_ROPE - half)]
    return jnp.concatenate([cq, ckv] + misc + [hy, sq, sk, sv, z, xbc], 1).astype(BF16)


def _mla_weights(w_uq, w_ukv):
    r = w_uq.shape[0]
    half = MLA_ROPE // 2
    qh = w_uq.reshape(r, MLA_HEADS, MLA_NOPE + MLA_ROPE)
    zq = jnp.zeros((r, MLA_HEADS, LANE - MLA_NOPE - MLA_ROPE - half), w_uq.dtype)
    wq = jnp.concatenate([qh, qh[..., MLA_NOPE:MLA_NOPE + half], zq], -1).reshape(r, MLA_HEADS * LANE)
    rk = w_ukv.shape[0]
    kvh = w_ukv.reshape(rk, MLA_HEADS, MLA_NOPE + MLA_V)
    zk = jnp.zeros((rk, MLA_HEADS, LANE - MLA_NOPE), w_ukv.dtype)
    wk = jnp.concatenate([kvh[..., :MLA_NOPE], zk], -1).reshape(rk, MLA_HEADS * LANE)
    v = kvh[..., MLA_NOPE:]
    zv = jnp.zeros_like(v)
    odd = (jnp.arange(MLA_HEADS) % 2 == 1)[None, :, None]
    wv = jnp.concatenate([jnp.where(odd, zv, v), jnp.where(odd, v, zv)], -1).reshape(rk, MLA_HEADS * LANE)
    return wq.astype(BF16), jnp.concatenate([wk, wv], 1).astype(BF16)


def _rope_tables(seq):
    inv_freq = ROPE_THETA ** (-jnp.arange(0, MLA_ROPE, 2, dtype=F32) / MLA_ROPE)
    ang = jnp.arange(seq, dtype=F32)[:, None] * inv_freq[None, :]
    cos, sin = jnp.cos(ang), jnp.sin(ang)
    one = jnp.ones((seq, MLA_NOPE), F32)
    zero = lambda n: jnp.zeros((seq, n), F32)
    ct = jnp.concatenate([one, cos, cos, zero(LANE - MLA_NOPE - MLA_ROPE)], 1)
    st = jnp.concatenate([zero(MLA_NOPE), -sin, sin, zero(LANE - MLA_NOPE - MLA_ROPE)], 1)
    return ct, st


def _hy_tables(seq):
    t = jnp.linspace(0.0, 1.0, seq, dtype=F32)[:, None]
    ang = 2.0 * math.pi * jnp.arange(seq, dtype=F32)[:, None] / seq
    bands = jnp.linspace(1e-4, HY_BANDS - 1, HY_BANDS, dtype=F32)[None, :]
    feat = jnp.concatenate([t, jnp.cos(bands * ang), -jnp.sin(bands * ang)], -1)
    max_decay = math.log(HY_TARGET) / HY_FAST_DECAY
    min_decay = math.log(HY_TARGET) / HY_SLOW_DECAY
    deltas = jnp.linspace(min_decay, max_decay, HY_WIDTH, dtype=F32)
    decay = jnp.exp(-t * jnp.abs(deltas)[None, :])
    return _pad_cols(feat, LANE), decay


def _pad2(w, rows, cols):
    return jnp.pad(w, ((0, rows - w.shape[0]), (0, cols - w.shape[1])))


def _hy_lag_filters(filt, seq):
    hf = filt.reshape(seq, HY_ORDER, 2, HY_WIDTH)
    fwd, bwd = hf[:, :, 0], hf[:, :, 1]
    k2 = jnp.concatenate([jnp.zeros_like(bwd[:1]), bwd[:0:-1], fwd], 0)
    return jnp.transpose(k2, (2, 1, 0))


def kernel(x, p, emb_ln_g, emb_ln_b, w_in, mla_q_norm, mla_kv_norm, mla_w_uq, mla_w_ukv, hy_conv_w, hy_conv_b, hy_f_w1, hy_f_b1, hy_f_freq, hy_f_w2, hy_f_b2, hy_f_w3, hy_bias, swa_sink, ssd_conv_w, ssd_conv_b, ssd_dt_bias, ssd_a_log, ssd_d, mix_norm_g, w_out, ln1_g, ln1_b, ffn_w_gate, ffn_w_up, ffn_conv_w, ffn_conv_b, ffn_w_down, ln2_g, ln2_b, ple_w_proj, ple_w_gate, ple_b_gate, ln3_g, ln3_b):
    bsz, seq, d = x.shape
    n = bsz * seq
    row = lambda v: v.reshape(1, -1)
    ct, st = _rope_tables(seq)
    feat, decay = _hy_tables(seq)
    h = x.reshape(n, d)
    for i in range(DEPTH):
        w_in_i = _in_proj_weight(w_in[i])
        if i == 0:
            h, mla_in, misc, hy_u, swa_in, ssd_in, dt_t = _in_proj(h, w_in_i, (row(emb_ln_g), row(emb_ln_b)))
        else:
            mla_in, misc, hy_u, swa_in, ssd_in, dt_t = _in_proj(h, w_in_i)

        wq, wkv = _mla_weights(mla_w_uq[i], mla_w_ukv[i])
        y_a = _mla(mla_in.reshape(bsz, seq, MLA_IN_W), misc.reshape(bsz, seq, MISC_W), ct, st,
                   row(mla_q_norm[i]), row(mla_kv_norm[i]), wq, wkv)

        filt = _hy_filter(feat, _pad2(hy_f_w1[i], LANE, LANE), _pad_cols(row(hy_f_b1[i]), LANE),
                          _pad_cols(row(hy_f_freq[i]), LANE), _pad2(hy_f_w2[i], LANE, LANE),
                          _pad_cols(row(hy_f_b2[i]), LANE), _pad2(hy_f_w3[i], LANE, 2 * HY_ORDER * HY_WIDTH),
                          decay)
        uct = _hy_pre(hy_u.reshape(bsz, seq, HY_IN_W), hy_conv_w[i], row(hy_conv_b[i]))
        y_bt = _hy_conv(uct.reshape(HY_IN_W, bsz, seq), _hy_lag_filters(filt, seq), hy_bias[i])

        y_c = _swa(swa_in.reshape(bsz, seq, SWA_IN_W), swa_sink[i])

        nch = seq // SSD_CHUNK
        dt_rows = jnp.transpose(dt_t.reshape(2 * SSD_HEADS, bsz, nch, SSD_CHUNK), (1, 2, 0, 3))
        dt_rows = dt_rows.reshape(bsz, nch * 2 * SSD_HEADS, SSD_CHUNK)
        per_row = lambda v: jnp.tile(v.reshape(-1, 1), (nch, 1))
        dskip = jnp.repeat(ssd_d[i], SSD_HEAD_DIM, axis=1)
        y_d = _ssd(ssd_in.reshape(bsz, seq, SSD_IN_W), dt_rows, ssd_conv_w[i], row(ssd_conv_b[i]),
                   per_row(ssd_dt_bias[i]), per_row(ssd_a_log[i]), dskip)

        flat = lambda y: y.reshape(n, GROUP_WIDTH)
        h = _out_proj(flat(y_a), y_bt.reshape(HY_WIDTH, n), flat(y_c), flat(y_d), h, row(mix_norm_g[i]),
                      w_out[i].astype(BF16), row(ln1_g[i]), row(ln1_b[i]))
        h = _ffn_ple(h, p[i].reshape(n, PLE_DIM), seq, ffn_w_gate[i].astype(BF16), ffn_w_up[i].astype(BF16),
                     ffn_conv_w[i], row(ffn_conv_b[i]), ffn_w_down[i].astype(BF16), row(ln2_g[i]), row(ln2_b[i]),
                     ple_w_proj[i].astype(BF16), ple_w_gate[i].astype(BF16), row(ple_b_gate[i]),
                     row(ln3_g[i]), row(ln3_b[i]))
    return h.reshape(bsz, seq, d)
```

```python
import functools
import math

import jax
import jax.numpy as jnp
import numpy as np
from jax import lax
from jax.experimental import pallas as pl
from jax.experimental.pallas import tpu as pltpu

F32 = jnp.float32
BF16 = jnp.bfloat16

D_MODEL = 1024
DEPTH = 2
PLE_DIM = 256
GROUP_WIDTH = 256
BLOCK = 128

MLA_HEADS = 4
MLA_Q_RANK = 256
MLA_KV_RANK = 128
MLA_NOPE = 64
MLA_ROPE = 32
MLA_V = 64
ROPE_THETA = 10000.0

HY_WIDTH = 256
HY_ORDER = 2
HY_EMB = 33
HY_BANDS = 16
HY_FAST_DECAY = 0.3
HY_SLOW_DECAY = 1.5
HY_TARGET = 1e-2
HY_BLK = 256

SWA_HEADS = 4
SWA_KV_HEADS = 2
SWA_HEAD_DIM = 64
SWA_WINDOW = 128

SSD_D_INNER = 256
SSD_HEAD_DIM = 64
SSD_HEADS = 4
SSD_GROUPS = 2
SSD_STATE = 128
SSD_CHUNK = 128

D_FF = 2816
LN_EPS = 1e-5
RMS_EPS = 1e-6
NEG_INF = -1e30
ALPHA = (2.0 * DEPTH) ** 0.25

LANE = 128
VMEM_LIMIT = 56 * 1024 * 1024

MLA_IN_W = 384
MISC_W = 128
HY_IN_W = 768
SWA_IN_W = 512
SSD_IN_W = 1024
IN_WIDTHS = (MLA_IN_W, MISC_W, HY_IN_W, SWA_IN_W, SSD_IN_W)
KR_LANE = MLA_NOPE

ROW_TILE = 512
FFN_HALO = 16
FFN_CHUNK = 1408


def _cparams(sem):
    return pltpu.CompilerParams(dimension_semantics=sem, vmem_limit_bytes=VMEM_LIMIT)


def _resident(shape):
    nd = len(shape)
    return pl.BlockSpec(shape, lambda *_: (0,) * nd, pipeline_mode=pl.Buffered(1))


def _layer_norm(x, g, b):
    mu = jnp.mean(x, -1, keepdims=True)
    xc = x - mu
    var = jnp.mean(xc * xc, -1, keepdims=True)
    return xc * lax.rsqrt(var + LN_EPS) * g + b


def _rms(x, g):
    return x * lax.rsqrt(jnp.mean(x * x, -1, keepdims=True) + RMS_EPS) * g


def _silu(x):
    return x / (1.0 + jnp.exp(-x))


def _dot(a, b):
    return jnp.dot(a, b, preferred_element_type=F32)


def _dot_nt(a, b):
    return lax.dot_general(a, b, (((1,), (1,)), ((), ())), preferred_element_type=F32)


def _dot_hi(a, b):
    return jnp.dot(a, b, preferred_element_type=F32, precision=lax.Precision.HIGHEST)


def _dot_nt_hi(a, b):
    return lax.dot_general(a, b, (((1,), (1,)), ((), ())), preferred_element_type=F32,
                           precision=lax.Precision.HIGHEST)


def _in_proj_kernel(apply_ln, *refs):
    if apply_ln:
        x_ref, g_ref, b_ref, w_ref, h_ref = refs[:5]
        out_refs = refs[5:]
        h = _layer_norm(x_ref[...], g_ref[...], b_ref[...])
        h_ref[...] = h
    else:
        x_ref, w_ref = refs[:2]
        out_refs = refs[2:]
        h = x_ref[...]
    u = _dot(h.astype(BF16), w_ref[...])
    o = 0
    for ref, w in zip(out_refs, IN_WIDTHS):
        ref[...] = u[:, o:o + w]
        o += w
    out_refs[-1][...] = u[:, MLA_IN_W:MLA_IN_W + MISC_W].T[0:2 * SSD_HEADS, :]


def _in_proj(x, w, ln=None):
    n = x.shape[0]
    row = lambda wd: pl.BlockSpec((ROW_TILE, wd), lambda i: (i, 0))
    outs = [jax.ShapeDtypeStruct((n, wd), F32) for wd in IN_WIDTHS]
    outs.append(jax.ShapeDtypeStruct((2 * SSD_HEADS, n), F32))
    out_specs = [row(wd) for wd in IN_WIDTHS] + [pl.BlockSpec((2 * SSD_HEADS, ROW_TILE), lambda i: (0, i))]
    if ln is not None:
        args = (x, ln[0], ln[1], w)
        in_specs = [row(D_MODEL), _resident((1, D_MODEL)), _resident((1, D_MODEL)), _resident(w.shape)]
        outs = [jax.ShapeDtypeStruct((n, D_MODEL), F32)] + outs
        out_specs = [row(D_MODEL)] + out_specs
    else:
        args = (x, w)
        in_specs = [row(D_MODEL), _resident(w.shape)]
    return pl.pallas_call(
        functools.partial(_in_proj_kernel, ln is not None),
        out_shape=outs, grid=(n // ROW_TILE,), in_specs=in_specs, out_specs=out_specs,
        compiler_params=_cparams(("parallel",)), name="in_proj")(*args)


MLA_TQ = 512


def _mla_kernel(x_ref, misc_ref, ct_ref, st_ref, gq_ref, gkv_ref, wq_ref, wkv_ref, o_ref,
                q_scr, k_scr, v_scr, e_scr, l_scr):
    seq = x_ref.shape[0]
    scale = (MLA_NOPE + MLA_ROPE) ** -0.5 * math.log2(math.e)
    half = MLA_ROPE // 2
    ct = ct_ref[...]
    st = st_ref[...]
    cqn = _rms(x_ref[:, 0:256], gq_ref[...]).astype(BF16)
    ckvn = _rms(x_ref[:, 256:384], gkv_ref[...]).astype(BF16)
    lane = lax.broadcasted_iota(jnp.int32, (1, LANE), 1)
    on_rope = (lane >= KR_LANE) & (lane < KR_LANE + MLA_ROPE)

    def rotate(x):
        return x * ct + pltpu.roll(x, LANE - half, 1) * st

    k_rope = jnp.where(on_rope, rotate(misc_ref[...]), 0.0)
    for h in range(MLA_HEADS):
        sl = slice(h * LANE, (h + 1) * LANE)
        q_scr[:, sl] = (rotate(_dot(cqn, wq_ref[:, sl])) * scale).astype(BF16)
        k_scr[:, sl] = (_dot(ckvn, wkv_ref[:, sl]) + k_rope).astype(BF16)
        vsl = slice(MLA_HEADS * LANE + h * LANE, MLA_HEADS * LANE + (h + 1) * LANE)
        v_scr[:, sl] = _dot(ckvn, wkv_ref[:, vsl]).astype(BF16)

    def tile_rows(t):
        start = t * MLA_TQ if isinstance(t, int) else pl.multiple_of(t * MLA_TQ, MLA_TQ)
        return pl.ds(start, MLA_TQ)

    def scores(t, h):
        sl = slice(h * LANE, (h + 1) * LANE)
        s = _dot_nt(q_scr[tile_rows(t), sl], k_scr[:, sl])
        e = jnp.exp2(s - jnp.max(s, -1, keepdims=True))
        l_scr[h % 2] = jnp.sum(e, -1, keepdims=True)
        e_scr[h % 2] = e.astype(BF16)

    def values(t, h):
        sl = slice(h * LANE, (h + 1) * LANE)
        o = _dot(e_scr[h % 2], v_scr[:, sl]) / l_scr[h % 2]
        out = (tile_rows(t), slice((h // 2) * LANE, (h // 2 + 1) * LANE))
        if h % 2 == 0:
            o_ref[out] = o
        else:
            o_ref[out] += o

    def q_tile(t, last):
        for h in range(MLA_HEADS):
            values(t, h)
            if h + 1 < MLA_HEADS:
                scores(t, h + 1)
            elif not last:
                scores(t + 1, 0)

    def body(t, carry):
        q_tile(t, False)
        return carry

    n_tiles = seq // MLA_TQ
    scores(0, 0)
    lax.fori_loop(0, n_tiles - 1, body, 0)
    q_tile(n_tiles - 1, True)


def _mla(mla_in, misc, ct, st, gq, gkv, wq, wkv):
    bsz, seq, _ = mla_in.shape
    return pl.pallas_call(
        _mla_kernel,
        out_shape=jax.ShapeDtypeStruct((bsz, seq, GROUP_WIDTH), F32),
        grid=(bsz,),
        in_specs=[pl.BlockSpec((None, seq, MLA_IN_W), lambda b: (b, 0, 0)),
                  pl.BlockSpec((None, seq, MISC_W), lambda b: (b, 0, 0)),
                  _resident(ct.shape), _resident(st.shape), _resident(gq.shape), _resident(gkv.shape),
                  _resident(wq.shape), _resident(wkv.shape)],
        out_specs=pl.BlockSpec((None, seq, GROUP_WIDTH), lambda b: (b, 0, 0)),
        scratch_shapes=[pltpu.VMEM((seq, MLA_HEADS * LANE), BF16)] * 3
                       + [pltpu.VMEM((2, MLA_TQ, seq), BF16), pltpu.VMEM((2, MLA_TQ, 1), F32)],
        compiler_params=_cparams(("parallel",)), name="mla")(mla_in, misc, ct, st, gq, gkv, wq, wkv)


def _hy_filter_kernel(feat_ref, w1_ref, b1_ref, fr_ref, w2_ref, b2_ref, w3f_ref, w3b_ref, decay_ref, o_ref):
    seq = feat_ref.shape[0] // 2
    fr = fr_ref[...]
    h = jnp.sin(fr * (_dot_hi(feat_ref[...], w1_ref[...]) + b1_ref[...]))
    h = jnp.sin(fr * (_dot_hi(h, w2_ref[...]) + b2_ref[...]))
    for o in range(HY_ORDER):
        rows = slice(o * HY_WIDTH, (o + 1) * HY_WIDTH)
        o_ref[o, :, 0:seq] = _dot_nt_hi(w3b_ref[rows, :], h[0:seq]) * decay_ref[:, 0:seq]
        o_ref[o, :, seq:2 * seq] = _dot_nt_hi(w3f_ref[rows, :], h[seq:2 * seq]) * decay_ref[:, seq:2 * seq]


def _hy_filter(feat, w1, b1, fr, w2, b2, w3f, w3b, decay):
    return pl.pallas_call(
        _hy_filter_kernel,
        out_shape=jax.ShapeDtypeStruct((HY_ORDER, HY_WIDTH, feat.shape[0]), F32),
        compiler_params=pltpu.CompilerParams(vmem_limit_bytes=VMEM_LIMIT),
        name="hy_filter")(feat, w1, b1, fr, w2, b2, w3f, w3b, decay)


def _shift_rows(x, row, seq):
    prev = jnp.where(row == 0, 0.0, pltpu.roll(x, 1, 0))
    nxt = jnp.where(row == seq - 1, 0.0, pltpu.roll(x, seq - 1, 0))
    return prev, nxt


def _dwconv3(x, w, b, row, seq):
    prev, nxt = _shift_rows(x, row, seq)
    return prev * w[0:1] + x * w[1:2] + nxt * w[2:3] + b


def _hy_pre_kernel(u_ref, w_ref, b_ref, o_ref):
    seq = u_ref.shape[0]
    row = lax.broadcasted_iota(jnp.int32, (seq, HY_WIDTH), 0)
    for k in range(HY_ORDER + 1):
        sl = slice(k * HY_WIDTH, (k + 1) * HY_WIDTH)
        o_ref[sl, :] = _dwconv3(u_ref[:, sl], w_ref[:, sl], b_ref[:, sl], row, seq).T


def _hy_pre(hy_u, w, b):
    bsz, seq, wd = hy_u.shape
    return pl.pallas_call(
        _hy_pre_kernel,
        out_shape=jax.ShapeDtypeStruct((wd, bsz * seq), F32),
        grid=(bsz,),
        in_specs=[pl.BlockSpec((None, seq, wd), lambda b: (b, 0, 0)), _resident(w.shape), _resident(b.shape)],
        out_specs=pl.BlockSpec((wd, seq), lambda b: (0, b)),
        compiler_params=_cparams(("parallel",)), name="hy_pre")(hy_u, w, b)


HY_ROWS16 = 16
HY_CH = 4


def _hy_conv_kernel(bias_ref, v_ref, x1_ref, x2_ref, f_ref, o_ref, r_scr):
    _, bsz, seq = v_ref.shape
    nblk = seq // HY_BLK
    c0 = pl.program_id(0) * HY_CH

    def build_toeplitz(slot, taps):
        k16 = jnp.broadcast_to(taps, (HY_ROWS16, 2 * seq))
        k16 = pltpu.roll(k16, 0, 1, stride=1, stride_axis=0)
        for g in range(LANE // HY_ROWS16):
            blk = (k16 if g == 0 else pltpu.roll(k16, g * HY_ROWS16, 1)).astype(BF16)
            r_scr[slot, g * HY_ROWS16:(g + 1) * HY_ROWS16, :] = blk
            r_scr[slot, LANE + g * HY_ROWS16:LANE + (g + 1) * HY_ROWS16, LANE:] = blk[:, :2 * seq - LANE]

    def long_conv(z, slot):
        zr = jnp.concatenate([z[:, j * HY_BLK:(j + 1) * HY_BLK] for j in range(nblk)], axis=0).astype(BF16)
        acc = [None] * nblk
        for d in range(-(nblk - 1), nblk):
            lo_in, hi_in = max(0, -d), nblk - max(0, d)
            w_d = r_scr[slot, :, seq + d * HY_BLK: seq + (d + 1) * HY_BLK]
            part = _dot(zr[lo_in * bsz:hi_in * bsz, :], w_d)
            for j in range(lo_in, hi_in):
                piece = part[(j - lo_in) * bsz:(j - lo_in + 1) * bsz, :]
                acc[j + d] = piece if acc[j + d] is None else acc[j + d] + piece
        return jnp.concatenate(acc, axis=1)

    for ch in range(HY_CH):
        for order in range(HY_ORDER):
            build_toeplitz(ch * HY_ORDER + order, f_ref[order, ch:ch + 1, :])
    for ch in range(HY_CH):
        z0 = v_ref[ch]
        z1 = x1_ref[ch] * (long_conv(z0, ch * HY_ORDER) + bias_ref[0, c0 + ch] * z0)
        o_ref[ch] = x2_ref[ch] * (long_conv(z1, ch * HY_ORDER + 1) + bias_ref[1, c0 + ch] * z1)


def _hy_conv(uct, filt, bias):
    _, bsz, seq = uct.shape
    per_slab = HY_WIDTH // HY_CH
    chan = lambda k: pl.BlockSpec((HY_CH, bsz, seq), lambda c, k=k: (c + k * per_slab, 0, 0))
    return pl.pallas_call(
        _hy_conv_kernel,
        out_shape=jax.ShapeDtypeStruct((HY_WIDTH, bsz, seq), F32),
        grid=(per_slab,),
        in_specs=[pl.BlockSpec(memory_space=pltpu.SMEM), chan(0), chan(1), chan(2),
                  pl.BlockSpec((None, HY_ORDER, HY_CH, 2 * seq), lambda c: (c, 0, 0, 0))],
        out_specs=pl.BlockSpec((HY_CH, bsz, seq), lambda c: (c, 0, 0)),
        scratch_shapes=[pltpu.VMEM((HY_CH * HY_ORDER, HY_BLK, 2 * seq), BF16)],
        compiler_params=_cparams(("parallel",)), name="hy_conv")(bias, uct, uct, uct, filt)


def _swa_kernel(sink_ref, x_ref, o_ref, k_scr, vlo_scr, vhi_scr, bias_scr):
    seq = x_ref.shape[0]
    nb = seq // BLOCK
    win = 3 * BLOCK
    grp = SWA_HEADS // SWA_KV_HEADS
    log2e = math.log2(math.e)
    scale = SWA_HEAD_DIM ** -0.5 * log2e
    kvw = SWA_KV_HEADS * LANE
    zeros = jnp.zeros((BLOCK, kvw), BF16)
    lane = lax.broadcasted_iota(jnp.int32, (1, LANE), 1)
    lo_half = lane < SWA_HEAD_DIM
    for scr in (k_scr, vlo_scr, vhi_scr):
        scr[0:BLOCK, :] = zeros
        scr[BLOCK + seq:2 * BLOCK + seq, :] = zeros
    body = slice(BLOCK, BLOCK + seq)
    k_both = x_ref[:, 256:256 + LANE]
    k_flip = pltpu.roll(k_both, SWA_HEAD_DIM, 1)
    k_scr[body, 0:LANE] = jnp.where(lo_half, k_both, k_flip).astype(BF16)
    k_scr[body, LANE:2 * LANE] = jnp.where(lo_half, k_flip, k_both).astype(BF16)
    v_both = x_ref[:, 256 + LANE:256 + 2 * LANE]
    v_flip = pltpu.roll(v_both, SWA_HEAD_DIM, 1)
    vlo_scr[body, 0:LANE] = jnp.where(lo_half, v_both, 0.0).astype(BF16)
    vhi_scr[body, 0:LANE] = jnp.where(lo_half, 0.0, v_flip).astype(BF16)
    vlo_scr[body, LANE:2 * LANE] = jnp.where(lo_half, v_flip, 0.0).astype(BF16)
    vhi_scr[body, LANE:2 * LANE] = jnp.where(lo_half, 0.0, v_both).astype(BF16)

    qi = lax.broadcasted_iota(jnp.int32, (grp * BLOCK, win), 0)
    ji = lax.broadcasted_iota(jnp.int32, (grp * BLOCK, win), 1)
    upper = qi >= BLOCK
    dist = jnp.abs(jnp.where(upper, qi - BLOCK, qi) + BLOCK - ji)
    slopes = [(2.0 ** (-8.0 / SWA_HEADS)) ** (i + 1) * log2e for i in range(SWA_HEADS)]
    for j in range(SWA_KV_HEADS):
        slope = jnp.where(upper, slopes[grp * j + 1], slopes[grp * j])
        bias_scr[j] = jnp.where(dist <= SWA_WINDOW, -slope * dist.astype(F32), NEG_INF)
    col = lax.broadcasted_iota(jnp.int32, (1, win), 1)
    row_up = lax.broadcasted_iota(jnp.int32, (grp * BLOCK, 1), 0) >= BLOCK

    def block(n, edge):
        r0 = n * BLOCK if isinstance(n, int) else pl.multiple_of(n * BLOCK, BLOCK)
        for j in range(SWA_KV_HEADS):
            sl = slice(j * LANE, (j + 1) * LANE)
            qp = x_ref[pl.ds(r0, BLOCK), sl] * scale
            q2 = jnp.concatenate([jnp.where(lo_half, qp, 0.0), jnp.where(lo_half, 0.0, qp)], 0).astype(BF16)
            sc = _dot_nt(q2, k_scr[pl.ds(r0, win), sl]) + bias_scr[j]
            if edge == "first":
                sc = jnp.where(col >= BLOCK, sc, NEG_INF)
            elif edge == "last":
                sc = jnp.where(col < 2 * BLOCK, sc, NEG_INF)
            sk = jnp.where(row_up, sink_ref[grp * j + 1], sink_ref[grp * j]) * log2e
            m = jnp.maximum(jnp.max(sc, -1, keepdims=True), sk)
            e = jnp.exp2(sc - m)
            den = jnp.sum(e, -1, keepdims=True) + jnp.exp2(sk - m)
            eb = e.astype(BF16)
            o_lo = _dot(eb[0:BLOCK], vlo_scr[pl.ds(r0, win), sl]) / den[0:BLOCK]
            o_hi = _dot(eb[BLOCK:], vhi_scr[pl.ds(r0, win), sl]) / den[BLOCK:]
            o_ref[pl.ds(r0, BLOCK), sl] = o_lo + o_hi

    def inner(n, carry):
        block(n, None)
        return carry

    block(0, "first")
    lax.fori_loop(1, nb - 1, inner, 0, unroll=2)
    block(nb - 1, "last")


def _swa(swa_in, sink):
    bsz, seq, _ = swa_in.shape
    kv_scr = pltpu.VMEM((seq + 2 * BLOCK, SWA_KV_HEADS * LANE), BF16)
    grp = SWA_HEADS // SWA_KV_HEADS
    return pl.pallas_call(
        _swa_kernel,
        out_shape=jax.ShapeDtypeStruct((bsz, seq, GROUP_WIDTH), F32),
        grid=(bsz,),
        in_specs=[pl.BlockSpec(memory_space=pltpu.SMEM),
                  pl.BlockSpec((None, seq, SWA_IN_W), lambda b: (b, 0, 0))],
        out_specs=pl.BlockSpec((None, seq, GROUP_WIDTH), lambda b: (b, 0, 0)),
        scratch_shapes=[kv_scr, kv_scr, kv_scr, pltpu.VMEM((SWA_KV_HEADS, grp * BLOCK, 3 * BLOCK), F32)],
        compiler_params=_cparams(("parallel",)), name="swa")(sink, swa_in)


def _softplus(x):
    return jnp.maximum(x, 0.0) + jnp.log1p(jnp.exp(-jnp.abs(x)))


def _split3(x):
    hi = x.astype(BF16)
    rest = x - hi.astype(F32)
    mid = rest.astype(BF16)
    return hi, mid, (rest - mid.astype(F32)).astype(BF16)


def _ssd_kernel(x_ref, dtr_ref, cw_ref, cb_ref, dtb_col_ref, alog_col_ref, dskip_ref, o_ref,
                act_scr, yf_scr, yb_scr, rows_scr, cols_scr):
    seq = x_ref.shape[0]
    q = SSD_CHUNK
    nch = seq // q
    nh = 2 * SSD_HEADS
    hpg = SSD_HEADS // SSD_GROUPS
    row = lax.broadcasted_iota(jnp.int32, (seq, GROUP_WIDTH), 0)
    for k in range(3):
        sl = slice(k * GROUP_WIDTH, (k + 1) * GROUP_WIDTH)
        xin = x_ref[:, 256 + k * GROUP_WIDTH:256 + (k + 1) * GROUP_WIDTH]
        act_scr[:, sl] = _silu(_dwconv3(xin, cw_ref[:, sl], cb_ref[:, sl], row, seq))

    ti = lax.broadcasted_iota(jnp.int32, (q, q), 0)
    si = lax.broadcasted_iota(jnp.int32, (q, q), 1)
    lower = si <= ti
    upper = si >= ti
    tri_l = jnp.where(lower, 1.0, 0.0).astype(BF16)
    tri_u = jnp.where(upper, 1.0, 0.0).astype(BF16)
    lane = lax.broadcasted_iota(jnp.int32, (1, LANE), 1)
    first = lane < SSD_HEAD_DIM

    dt_all = _softplus(dtr_ref[...] + dtb_col_ref[...])
    da_parts = _split3(dt_all * -jnp.exp(alog_col_ref[...]))
    row_tri = jnp.concatenate([tri_u, tri_l], axis=1)
    col_tri = jnp.concatenate([tri_l, tri_u], axis=0)
    rows_scr[...] = sum(_dot(part, row_tri) for part in da_parts)
    cols_scr[0:2 * q, :] = sum(_dot_nt(col_tri, part) for part in da_parts)
    cols_scr[2 * q:3 * q, :] = dt_all.T

    def per_head(col, h0):
        return jnp.where(first, col[:, h0:h0 + 1], col[:, h0 + 1:h0 + 2])

    def one_chunk(direction, c, carry, y_scr):
        hb = direction * SSD_HEADS
        keep = lower if direction == 0 else upper
        edge = q - 1 if direction == 0 else 0
        rows = slice(c * q, (c + 1) * q)
        heads = slice(c * nh, (c + 1) * nh)
        cs_col = cols_scr[direction * q:(direction + 1) * q, heads]
        cs_row = rows_scr[heads, direction * q:(direction + 1) * q]
        dt_col = cols_scr[2 * q:3 * q, heads]
        new = []
        for g in range(SSD_GROUPS):
            gs = slice(g * LANE, (g + 1) * LANE)
            h0 = hb + g * hpg
            xg = act_scr[rows, gs]
            bg = act_scr[rows, GROUP_WIDTH + g * LANE:GROUP_WIDTH + (g + 1) * LANE]
            cg = act_scr[rows, 2 * GROUP_WIDTH + g * LANE:2 * GROUP_WIDTH + (g + 1) * LANE]
            xdt = xg * per_head(dt_col, h0)
            cb = _dot_nt(cg.astype(BF16), bg.astype(BF16))
            y = None
            for r in range(hpg):
                seg = cs_col[:, h0 + r:h0 + r + 1] - cs_row[h0 + r:h0 + r + 1, :]
                lmat = jnp.exp(jnp.where(keep, seg, NEG_INF))
                xh = jnp.where(first if r == 0 else ~first, xdt, 0.0)
                part = _dot((cb * lmat).astype(BF16), xh.astype(BF16))
                y = part if y is None else y + part
            cs_g = per_head(cs_col, h0)
            total = cs_g[edge:edge + 1, :]
            y_scr[rows, gs] = y + _dot(cg.astype(BF16), carry[g].astype(BF16)) * jnp.exp(cs_g)
            st = _dot(bg.T.astype(BF16), (xdt * jnp.exp(total - cs_g)).astype(BF16))
            new.append(carry[g] * jnp.exp(total) + st)
        return new

    fwd = [jnp.zeros((SSD_STATE, LANE), F32) for _ in range(SSD_GROUPS)]
    bwd = [jnp.zeros((SSD_STATE, LANE), F32) for _ in range(SSD_GROUPS)]
    for i in range(nch):
        fwd = one_chunk(0, i, fwd, yf_scr)
        bwd = one_chunk(1, nch - 1 - i, bwd, yb_scr)

    dsum = dskip_ref[0:1, :] + dskip_ref[1:2, :]
    y = yf_scr[...] + yb_scr[...] + dsum * act_scr[:, 0:GROUP_WIDTH]
    o_ref[...] = y * _silu(x_ref[:, 0:GROUP_WIDTH])


def _ssd(ssd_in, dt_rows, cw, cb, dtb_col, alog_col, dskip):
    bsz, seq, _ = ssd_in.shape
    q = SSD_CHUNK
    nch = seq // q
    nh = 2 * SSD_HEADS
    small = (cw, cb, dtb_col, alog_col, dskip)
    return pl.pallas_call(
        _ssd_kernel,
        out_shape=jax.ShapeDtypeStruct((bsz, seq, GROUP_WIDTH), F32),
        grid=(bsz,),
        in_specs=[pl.BlockSpec((None, seq, SSD_IN_W), lambda b: (b, 0, 0)),
                  pl.BlockSpec((None, nch * nh, q), lambda b: (b, 0, 0))]
                 + [_resident(a.shape) for a in small],
        out_specs=pl.BlockSpec((None, seq, GROUP_WIDTH), lambda b: (b, 0, 0)),
        scratch_shapes=[pltpu.VMEM((seq, 3 * GROUP_WIDTH), F32)] + [pltpu.VMEM((seq, GROUP_WIDTH), F32)] * 2
                       + [pltpu.VMEM((nch * nh, 2 * q), F32), pltpu.VMEM((3 * q, nch * nh), F32)],
        compiler_params=_cparams(("parallel",)), name="ssd")(ssd_in, dt_rows, *small)


def _out_proj_kernel(ya_ref, yb_ref, yc_ref, yd_ref, h_ref, g_ref, w_ref, lg_ref, lb_ref, o_ref):
    parts = []
    for k, y in enumerate((ya_ref[...], yb_ref[...].T, yc_ref[...], yd_ref[...])):
        parts.append(_rms(y, g_ref[:, k * GROUP_WIDTH:(k + 1) * GROUP_WIDTH]).astype(BF16))
    y = jnp.concatenate(parts, axis=1)
    o_ref[...] = _layer_norm(ALPHA * h_ref[...] + _dot(y, w_ref[...]), lg_ref[...], lb_ref[...])


def _out_proj(ya, ybt, yc, yd, h, g, w, lg, lb):
    n = h.shape[0]
    row = lambda wd: pl.BlockSpec((ROW_TILE, wd), lambda i: (i, 0))
    col = pl.BlockSpec((GROUP_WIDTH, ROW_TILE), lambda i: (0, i))
    return pl.pallas_call(
        _out_proj_kernel,
        out_shape=jax.ShapeDtypeStruct((n, D_MODEL), F32),
        grid=(n // ROW_TILE,),
        in_specs=[row(GROUP_WIDTH), col, row(GROUP_WIDTH), row(GROUP_WIDTH), row(D_MODEL), _resident(g.shape),
                  _resident(w.shape), _resident(lg.shape), _resident(lb.shape)],
        out_specs=row(D_MODEL),
        compiler_params=_cparams(("parallel",)), name="out_proj")(ya, ybt, yc, yd, h, g, w, lg, lb)


def _ffn_kernel(tiles_per_seq, hp_ref, h_ref, hn_ref, p_ref, wg_ref, wu_ref, cw_ref, cb_ref, wd_ref,
                lg_ref, lb_ref, pwp_ref, pwg_ref, pbg_ref, plg_ref, plb_ref, o_ref):
    i = pl.program_id(0)
    ext = ROW_TILE + 2 * FFN_HALO
    h = h_ref[...]
    hb = h.astype(BF16)
    hext = jnp.concatenate([hp_ref[...].astype(BF16), hb, hn_ref[...].astype(BF16)], axis=0)
    row = lax.broadcasted_iota(jnp.int32, (ROW_TILE, 1), 0)
    keep_prev = jnp.where((row == 0) & (i % tiles_per_seq == 0), 0.0, 1.0)
    keep_next = jnp.where((row == ROW_TILE - 1) & (i % tiles_per_seq == tiles_per_seq - 1), 0.0, 1.0)
    f = None
    for c0 in range(0, D_FF, FFN_CHUNK):
        sl = slice(c0, c0 + FFN_CHUNK)
        gate = _dot(hext, wg_ref[:, sl])
        g_prev = pltpu.roll(gate, 1, 0)[FFN_HALO:FFN_HALO + ROW_TILE]
        g_next = pltpu.roll(gate, ext - 1, 0)[FFN_HALO:FFN_HALO + ROW_TILE]
        g_cur = gate[FFN_HALO:FFN_HALO + ROW_TILE]
        conv = (g_prev * keep_prev * cw_ref[0:1, sl] + g_cur * cw_ref[1:2, sl]
                + g_next * keep_next * cw_ref[2:3, sl] + cb_ref[:, sl])
        act = (_silu(conv) * _dot(hb, wu_ref[:, sl])).astype(BF16)
        part = _dot(act, wd_ref[sl, :])
        f = part if f is None else f + part
    h = _layer_norm(ALPHA * h + f, lg_ref[...], lb_ref[...])
    gate = _dot(h.astype(BF16), pwg_ref[...]) + pbg_ref[...]
    e = _dot(p_ref[...].astype(BF16), pwp_ref[...]) / (1.0 + jnp.exp(-gate))
    o_ref[...] = _layer_norm(ALPHA * h + e, plg_ref[...], plb_ref[...])


def _ffn_ple(h, p, layer, seq, wg, wu, cw, cb, wd, lg, lb, pwp, pwg, pbg, plg, plb):
    n = h.shape[0]
    per_tile = ROW_TILE // FFN_HALO
    last = n // FFN_HALO - 1
    small = (wg, wu, cw, cb, wd, lg, lb, pwp, pwg, pbg, plg, plb)
    return pl.pallas_call(
        functools.partial(_ffn_kernel, seq // ROW_TILE),
        out_shape=jax.ShapeDtypeStruct((n, D_MODEL), F32),
        grid=(n // ROW_TILE,),
        in_specs=[pl.BlockSpec((FFN_HALO, D_MODEL), lambda i: (jnp.maximum(i * per_tile - 1, 0), 0)),
                  pl.BlockSpec((ROW_TILE, D_MODEL), lambda i: (i, 0)),
                  pl.BlockSpec((FFN_HALO, D_MODEL), lambda i: (jnp.minimum((i + 1) * per_tile, last), 0)),
                  pl.BlockSpec((None, ROW_TILE, PLE_DIM), lambda i: (layer, i, 0))]
                 + [_resident(a.shape) for a in small],
        out_specs=pl.BlockSpec((ROW_TILE, D_MODEL), lambda i: (i, 0)),
        compiler_params=_cparams(("parallel",)), name="ffn_ple")(h, h, h, p, *small)


def _pad_cols(w, width):
    return jnp.pad(w, ((0, 0), (0, width - w.shape[1])))


def _in_proj_weight(w):
    o = np.cumsum([0, 256, 128, 32, 768, 256, 128, 128, 256, 768, 8])
    cq, ckv, kr, hy, sq, sk, sv, z, xbc, dt = (w[:, o[i]:o[i + 1]] for i in range(10))
    zc = lambda n: jnp.zeros((w.shape[0], n), w.dtype)
    half = MLA_ROPE // 2
    misc = [dt, zc(KR_LANE - dt.shape[1]), kr, kr[:, :half], zc(MISC_W - KR_LANE - MLA_ROPE - half)]
    return jnp.concatenate([cq, ckv] + misc + [hy, sq, sk, sv, z, xbc], 1).astype(BF16)


def _mla_weights(w_uq, w_ukv):
    r = w_uq.shape[0]
    half = MLA_ROPE // 2
    qh = w_uq.reshape(r, MLA_HEADS, MLA_NOPE + MLA_ROPE)
    zq = jnp.zeros((r, MLA_HEADS, LANE - MLA_NOPE - MLA_ROPE - half), w_uq.dtype)
    wq = jnp.concatenate([qh, qh[..., MLA_NOPE:MLA_NOPE + half], zq], -1).reshape(r, MLA_HEADS * LANE)
    rk = w_ukv.shape[0]
    kvh = w_ukv.reshape(rk, MLA_HEADS, MLA_NOPE + MLA_V)
    zk = jnp.zeros((rk, MLA_HEADS, LANE - MLA_NOPE), w_ukv.dtype)
    wk = jnp.concatenate([kvh[..., :MLA_NOPE], zk], -1).reshape(rk, MLA_HEADS * LANE)
    v = kvh[..., MLA_NOPE:]
    zv = jnp.zeros_like(v)
    odd = (jnp.arange(MLA_HEADS) % 2 == 1)[None, :, None]
    wv = jnp.concatenate([jnp.where(odd, zv, v), jnp.where(odd, v, zv)], -1).reshape(rk, MLA_HEADS * LANE)
    return wq.astype(BF16), jnp.concatenate([wk, wv], 1).astype(BF16)


def _rope_tables(seq):
    inv_freq = ROPE_THETA ** (-jnp.arange(0, MLA_ROPE, 2, dtype=F32) / MLA_ROPE)
    ang = jnp.arange(seq, dtype=F32)[:, None] * inv_freq[None, :]
    cos, sin = jnp.cos(ang), jnp.sin(ang)
    one = jnp.ones((seq, MLA_NOPE), F32)
    zero = lambda n: jnp.zeros((seq, n), F32)
    ct = jnp.concatenate([one, cos, cos, zero(LANE - MLA_NOPE - MLA_ROPE)], 1)
    st = jnp.concatenate([zero(MLA_NOPE), -sin, sin, zero(LANE - MLA_NOPE - MLA_ROPE)], 1)
    return ct, st


def _hy_tables(seq):
    t = jnp.linspace(0.0, 1.0, seq, dtype=F32)[:, None]
    ang = 2.0 * math.pi * jnp.arange(seq, dtype=F32)[:, None] / seq
    bands = jnp.linspace(1e-4, HY_BANDS - 1, HY_BANDS, dtype=F32)[None, :]
    feat = jnp.concatenate([t, jnp.cos(bands * ang), -jnp.sin(bands * ang)], -1)
    max_decay = math.log(HY_TARGET) / HY_FAST_DECAY
    min_decay = math.log(HY_TARGET) / HY_SLOW_DECAY
    deltas = jnp.linspace(min_decay, max_decay, HY_WIDTH, dtype=F32)
    decay = jnp.exp(-t * jnp.abs(deltas)[None, :])
    m = jnp.arange(2 * seq)
    pos = jnp.where(m < seq, jnp.minimum(seq - m, seq - 1), m - seq)
    decay_lag = jnp.where((m == 0)[:, None], 0.0, decay[pos])
    return _pad_cols(feat[pos], LANE), decay_lag.T


def _pad2(w, rows, cols):
    return jnp.pad(w, ((0, rows - w.shape[0]), (0, cols - w.shape[1])))


def _hy_tap_weights(w3):
    w = w3.reshape(w3.shape[0], HY_ORDER, 2, HY_WIDTH)
    rows = lambda d: _pad_cols(w[:, :, d].reshape(w3.shape[0], HY_ORDER * HY_WIDTH).T, LANE)
    return rows(0), rows(1)


def kernel(x, p, emb_ln_g, emb_ln_b, w_in, mla_q_norm, mla_kv_norm, mla_w_uq, mla_w_ukv, hy_conv_w, hy_conv_b, hy_f_w1, hy_f_b1, hy_f_freq, hy_f_w2, hy_f_b2, hy_f_w3, hy_bias, swa_sink, ssd_conv_w, ssd_conv_b, ssd_dt_bias, ssd_a_log, ssd_d, mix_norm_g, w_out, ln1_g, ln1_b, ffn_w_gate, ffn_w_up, ffn_conv_w, ffn_conv_b, ffn_w_down, ln2_g, ln2_b, ple_w_proj, ple_w_gate, ple_b_gate, ln3_g, ln3_b):
    bsz, seq, d = x.shape
    n = bsz * seq
    row = lambda v: v.reshape(1, -1)
    ct, st = _rope_tables(seq)
    feat, decay = _hy_tables(seq)
    h = x.reshape(n, d)
    for i in range(DEPTH):
        w_in_i = _in_proj_weight(w_in[i])
        if i == 0:
            h, mla_in, misc, hy_u, swa_in, ssd_in, dt_t = _in_proj(h, w_in_i, (row(emb_ln_g), row(emb_ln_b)))
        else:
            mla_in, misc, hy_u, swa_in, ssd_in, dt_t = _in_proj(h, w_in_i)

        wq, wkv = _mla_weights(mla_w_uq[i], mla_w_ukv[i])
        y_a = _mla(mla_in.reshape(bsz, seq, MLA_IN_W), misc.reshape(bsz, seq, MISC_W), ct, st,
                   row(mla_q_norm[i]), row(mla_kv_norm[i]), wq, wkv)

        w3f, w3b = _hy_tap_weights(hy_f_w3[i])
        filt = _hy_filter(feat, _pad2(hy_f_w1[i], LANE, LANE), _pad_cols(row(hy_f_b1[i]), LANE),
                          _pad_cols(row(hy_f_freq[i]), LANE), _pad2(hy_f_w2[i], LANE, LANE),
                          _pad_cols(row(hy_f_b2[i]), LANE), w3f, w3b, decay)
        filt = jnp.transpose(filt.reshape(HY_ORDER, HY_WIDTH // HY_CH, HY_CH, 2 * seq), (1, 0, 2, 3))
        uct = _hy_pre(hy_u.reshape(bsz, seq, HY_IN_W), hy_conv_w[i], row(hy_conv_b[i]))
        y_bt = _hy_conv(uct.reshape(HY_IN_W, bsz, seq), filt, hy_bias[i])

        y_c = _swa(swa_in.reshape(bsz, seq, SWA_IN_W), swa_sink[i])

        nch = seq // SSD_CHUNK
        dt_rows = jnp.transpose(dt_t.reshape(2 * SSD_HEADS, bsz, nch, SSD_CHUNK), (1, 2, 0, 3))
        dt_rows = dt_rows.reshape(bsz, nch * 2 * SSD_HEADS, SSD_CHUNK)
        per_row = lambda v: jnp.tile(v.reshape(-1, 1), (nch, 1))
        dskip = jnp.repeat(ssd_d[i], SSD_HEAD_DIM, axis=1)
        y_d = _ssd(ssd_in.reshape(bsz, seq, SSD_IN_W), dt_rows, ssd_conv_w[i], row(ssd_conv_b[i]),
                   per_row(ssd_dt_bias[i]), per_row(ssd_a_log[i]), dskip)

        flat = lambda y: y.reshape(n, GROUP_WIDTH)
        h = _out_proj(flat(y_a), y_bt.reshape(HY_WIDTH, n), flat(y_c), flat(y_d), h, row(mix_norm_g[i]),
                      w_out[i].astype(BF16), row(ln1_g[i]), row(ln1_b[i]))
        h = _ffn_ple(h, p.reshape(DEPTH, n, PLE_DIM), i, seq, ffn_w_gate[i].astype(BF16), ffn_w_up[i].astype(BF16),
                     ffn_conv_w[i], row(ffn_conv_b[i]), ffn_w_down[i].astype(BF16), row(ln2_g[i]), row(ln2_b[i]),
                     ple_w_proj[i].astype(BF16), ple_w_gate[i].astype(BF16), row(ple_b_gate[i]),
                     row(ln3_g[i]), row(ln3_b[i]))
    return h.reshape(bsz, seq, d)
```

```python
import functools
import math

import jax
import jax.numpy as jnp
import numpy as np
from jax import lax
from jax.experimental import pallas as pl
from jax.experimental.pallas import tpu as pltpu

F32 = jnp.float32
BF16 = jnp.bfloat16

D_MODEL = 1024
DEPTH = 2
PLE_DIM = 256
GROUP_WIDTH = 256
N_MIX = 4
HYENA_SLOT = 1
BLOCK = 128

MLA_HEADS = 4
MLA_Q_RANK = 256
MLA_KV_RANK = 128
MLA_NOPE = 64
MLA_ROPE = 32
MLA_V = 64
ROPE_THETA = 10000.0

HY_WIDTH = 256
HY_ORDER = 2
HY_EMB = 33
HY_BANDS = 16
HY_FAST_DECAY = 0.3
HY_SLOW_DECAY = 1.5
HY_TARGET = 1e-2
HY_BLK = 256

SWA_HEADS = 4
SWA_KV_HEADS = 2
SWA_HEAD_DIM = 64
SWA_WINDOW = 128
SWA_QB = 256

SSD_D_INNER = 256
SSD_HEAD_DIM = 64
SSD_HEADS = 4
SSD_GROUPS = 2
SSD_STATE = 128
SSD_CHUNK = 128

D_FF = 2816
LN_EPS = 1e-5
RMS_EPS = 1e-6
NEG_INF = -1e30
ALPHA = (2.0 * DEPTH) ** 0.25

LANE = 128
SUBLANE = 8
VMEM_LIMIT = 56 * 1024 * 1024

MLA_IN_W = 384
MISC_W = 128
HY_IN_W = 768
SWA_IN_W = 512
SSD_IN_W = 1024
IN_WIDTHS = (MLA_IN_W, MISC_W, HY_IN_W, SWA_IN_W, SSD_IN_W)
KR_LANE = MLA_NOPE

ROW_TILE = 512
FFN_HALO = 16
FFN_CHUNK = 1408


def _cparams(sem):
    return pltpu.CompilerParams(dimension_semantics=sem, vmem_limit_bytes=VMEM_LIMIT)


def _resident(shape):
    nd = len(shape)
    return pl.BlockSpec(shape, lambda *_: (0,) * nd, pipeline_mode=pl.Buffered(1))


def _layer_norm(x, g, b):
    mu = jnp.mean(x, -1, keepdims=True)
    xc = x - mu
    var = jnp.mean(xc * xc, -1, keepdims=True)
    return xc * lax.rsqrt(var + LN_EPS) * g + b


def _rms(x, g):
    return x * lax.rsqrt(jnp.mean(x * x, -1, keepdims=True) + RMS_EPS) * g


def _silu(x):
    return x / (1.0 + jnp.exp(-x))


def _dot(a, b):
    return jnp.dot(a, b, preferred_element_type=F32)


def _dot_nt(a, b):
    return lax.dot_general(a, b, (((1,), (1,)), ((), ())), preferred_element_type=F32)


def _dot_hi(a, b):
    return jnp.dot(a, b, preferred_element_type=F32, precision=lax.Precision.HIGHEST)


def _dot_nt_hi(a, b):
    return lax.dot_general(a, b, (((1,), (1,)), ((), ())), preferred_element_type=F32,
                           precision=lax.Precision.HIGHEST)


def _in_proj_kernel(apply_ln, *refs):
    if apply_ln:
        x_ref, g_ref, b_ref, w_ref, h_ref = refs[:5]
        out_refs = refs[5:]
        h = _layer_norm(x_ref[...], g_ref[...], b_ref[...])
        h_ref[...] = h
    else:
        x_ref, w_ref = refs[:2]
        out_refs = refs[2:]
        h = x_ref[...]
    u = _dot(h.astype(BF16), w_ref[...])
    o = 0
    for ref, w in zip(out_refs, IN_WIDTHS):
        ref[...] = u[:, o:o + w]
        o += w
    out_refs[-1][...] = u[:, MLA_IN_W:MLA_IN_W + MISC_W].T[0:2 * SSD_HEADS, :]


def _in_proj(x, w, ln=None):
    n = x.shape[0]
    row = lambda wd: pl.BlockSpec((ROW_TILE, wd), lambda i: (i, 0))
    outs = [jax.ShapeDtypeStruct((n, wd), F32) for wd in IN_WIDTHS]
    outs.append(jax.ShapeDtypeStruct((2 * SSD_HEADS, n), F32))
    out_specs = [row(wd) for wd in IN_WIDTHS] + [pl.BlockSpec((2 * SSD_HEADS, ROW_TILE), lambda i: (0, i))]
    if ln is not None:
        args = (x, ln[0], ln[1], w)
        in_specs = [row(D_MODEL), _resident((1, D_MODEL)), _resident((1, D_MODEL)), _resident(w.shape)]
        outs = [jax.ShapeDtypeStruct((n, D_MODEL), F32)] + outs
        out_specs = [row(D_MODEL)] + out_specs
    else:
        args = (x, w)
        in_specs = [row(D_MODEL), _resident(w.shape)]
    return pl.pallas_call(
        functools.partial(_in_proj_kernel, ln is not None),
        out_shape=outs, grid=(n // ROW_TILE,), in_specs=in_specs, out_specs=out_specs,
        compiler_params=_cparams(("parallel",)), name="in_proj")(*args)


MLA_TQ = 512


def _mla_kernel(x_ref, misc_ref, ct_ref, st_ref, gq_ref, gkv_ref, wq_ref, wkv_ref, o_ref,
                q_scr, k_scr, v_scr, e_scr, l_scr):
    seq = x_ref.shape[0]
    scale = (MLA_NOPE + MLA_ROPE) ** -0.5 * math.log2(math.e)
    half = MLA_ROPE // 2
    ct = ct_ref[...]
    st = st_ref[...]
    cqn = _rms(x_ref[:, 0:256], gq_ref[...]).astype(BF16)
    ckvn = _rms(x_ref[:, 256:384], gkv_ref[...]).astype(BF16)
    lane = lax.broadcasted_iota(jnp.int32, (1, LANE), 1)
    on_rope = (lane >= KR_LANE) & (lane < KR_LANE + MLA_ROPE)

    def rotate(x):
        return x * ct + pltpu.roll(x, LANE - half, 1) * st

    k_rope = jnp.where(on_rope, rotate(misc_ref[...]), 0.0)
    for h in range(MLA_HEADS):
        sl = slice(h * LANE, (h + 1) * LANE)
        q_scr[:, sl] = (rotate(_dot(cqn, wq_ref[:, sl])) * scale).astype(BF16)
        k_scr[:, sl] = (_dot(ckvn, wkv_ref[:, sl]) + k_rope).astype(BF16)
        vsl = slice(MLA_HEADS * LANE + h * LANE, MLA_HEADS * LANE + (h + 1) * LANE)
        v_scr[:, sl] = _dot(ckvn, wkv_ref[:, vsl]).astype(BF16)

    def tile_rows(t):
        start = t * MLA_TQ if isinstance(t, int) else pl.multiple_of(t * MLA_TQ, MLA_TQ)
        return pl.ds(start, MLA_TQ)

    def scores(t, h):
        sl = slice(h * LANE, (h + 1) * LANE)
        s = _dot_nt(q_scr[tile_rows(t), sl], k_scr[:, sl])
        e = jnp.exp2(s - jnp.max(s, -1, keepdims=True))
        l_scr[h % 2] = jnp.sum(e, -1, keepdims=True)
        e_scr[h % 2] = e.astype(BF16)

    def values(t, h):
        sl = slice(h * LANE, (h + 1) * LANE)
        o = _dot(e_scr[h % 2], v_scr[:, sl]) / l_scr[h % 2]
        out = (tile_rows(t), slice((h // 2) * LANE, (h // 2 + 1) * LANE))
        if h % 2 == 0:
            o_ref[out] = o
        else:
            o_ref[out] += o

    def q_tile(t, last):
        for h in range(MLA_HEADS):
            values(t, h)
            if h + 1 < MLA_HEADS:
                scores(t, h + 1)
            elif not last:
                scores(t + 1, 0)

    def body(t, carry):
        q_tile(t, False)
        return carry

    n_tiles = seq // MLA_TQ
    scores(0, 0)
    lax.fori_loop(0, n_tiles - 1, body, 0)
    q_tile(n_tiles - 1, True)


def _mla(mla_in, misc, ct, st, gq, gkv, wq, wkv):
    bsz, seq, _ = mla_in.shape
    return pl.pallas_call(
        _mla_kernel,
        out_shape=jax.ShapeDtypeStruct((bsz, seq, GROUP_WIDTH), F32),
        grid=(bsz,),
        in_specs=[pl.BlockSpec((None, seq, MLA_IN_W), lambda b: (b, 0, 0)),
                  pl.BlockSpec((None, seq, MISC_W), lambda b: (b, 0, 0)),
                  _resident(ct.shape), _resident(st.shape), _resident(gq.shape), _resident(gkv.shape),
                  _resident(wq.shape), _resident(wkv.shape)],
        out_specs=pl.BlockSpec((None, seq, GROUP_WIDTH), lambda b: (b, 0, 0)),
        scratch_shapes=[pltpu.VMEM((seq, MLA_HEADS * LANE), BF16)] * 3
                       + [pltpu.VMEM((2, MLA_TQ, seq), BF16), pltpu.VMEM((2, MLA_TQ, 1), F32)],
        compiler_params=_cparams(("parallel",)), name="mla")(mla_in, misc, ct, st, gq, gkv, wq, wkv)


def _hy_filter_kernel(feat_ref, w1_ref, b1_ref, fr_ref, w2_ref, b2_ref, w3f_ref, w3b_ref, decay_ref, o_ref):
    seq = feat_ref.shape[0] // 2
    fr = fr_ref[...]
    h = jnp.sin(fr * (_dot_hi(feat_ref[...], w1_ref[...]) + b1_ref[...]))
    h = jnp.sin(fr * (_dot_hi(h, w2_ref[...]) + b2_ref[...]))
    for o in range(HY_ORDER):
        rows = slice(o * HY_WIDTH, (o + 1) * HY_WIDTH)
        o_ref[o, :, 0:seq] = _dot_nt_hi(w3b_ref[rows, :], h[0:seq]) * decay_ref[:, 0:seq]
        o_ref[o, :, seq:2 * seq] = _dot_nt_hi(w3f_ref[rows, :], h[seq:2 * seq]) * decay_ref[:, seq:2 * seq]


def _hy_filter(feat, w1, b1, fr, w2, b2, w3f, w3b, decay):
    return pl.pallas_call(
        _hy_filter_kernel,
        out_shape=jax.ShapeDtypeStruct((HY_ORDER, HY_WIDTH, feat.shape[0]), F32),
        compiler_params=pltpu.CompilerParams(vmem_limit_bytes=VMEM_LIMIT),
        name="hy_filter")(feat, w1, b1, fr, w2, b2, w3f, w3b, decay)


def _shift_rows(x, row, seq):
    prev = jnp.where(row == 0, 0.0, pltpu.roll(x, 1, 0))
    nxt = jnp.where(row == seq - 1, 0.0, pltpu.roll(x, seq - 1, 0))
    return prev, nxt


def _dwconv3(x, w, b, row, seq):
    prev, nxt = _shift_rows(x, row, seq)
    return prev * w[0:1] + x * w[1:2] + nxt * w[2:3] + b


def _hy_pre_kernel(u_ref, w_ref, b_ref, o_ref):
    seq = u_ref.shape[0]
    row = lax.broadcasted_iota(jnp.int32, (seq, HY_WIDTH), 0)
    for k in range(HY_ORDER + 1):
        sl = slice(k * HY_WIDTH, (k + 1) * HY_WIDTH)
        o_ref[sl, :] = _dwconv3(u_ref[:, sl], w_ref[:, sl], b_ref[:, sl], row, seq).T


def _hy_pre(hy_u, w, b):
    bsz, seq, wd = hy_u.shape
    return pl.pallas_call(
        _hy_pre_kernel,
        out_shape=jax.ShapeDtypeStruct((wd, bsz * seq), F32),
        grid=(bsz,),
        in_specs=[pl.BlockSpec((None, seq, wd), lambda b: (b, 0, 0)), _resident(w.shape), _resident(b.shape)],
        out_specs=pl.BlockSpec((wd, seq), lambda b: (0, b)),
        compiler_params=_cparams(("parallel",)), name="hy_pre")(hy_u, w, b)


HY_ROWS16 = 16
HY_CH = 4


def _hy_conv_kernel(bias_ref, v_ref, x1_ref, x2_ref, f_ref, o_ref, r_scr):
    _, bsz, seq = v_ref.shape
    nblk = seq // HY_BLK
    c0 = pl.program_id(0) * HY_CH

    def build_toeplitz(slot, taps):
        k16 = jnp.broadcast_to(taps, (HY_ROWS16, 2 * seq))
        k16 = pltpu.roll(k16, 0, 1, stride=1, stride_axis=0)
        for g in range(LANE // HY_ROWS16):
            blk = (k16 if g == 0 else pltpu.roll(k16, g * HY_ROWS16, 1)).astype(BF16)
            r_scr[slot, g * HY_ROWS16:(g + 1) * HY_ROWS16, :] = blk
            r_scr[slot, LANE + g * HY_ROWS16:LANE + (g + 1) * HY_ROWS16, LANE:] = blk[:, :2 * seq - LANE]

    def long_conv(z, slot):
        zr = jnp.concatenate([z[:, j * HY_BLK:(j + 1) * HY_BLK] for j in range(nblk)], axis=0).astype(BF16)
        acc = [None] * nblk
        for d in range(-(nblk - 1), nblk):
            lo_in, hi_in = max(0, -d), nblk - max(0, d)
            w_d = r_scr[slot, :, seq + d * HY_BLK: seq + (d + 1) * HY_BLK]
            part = _dot(zr[lo_in * bsz:hi_in * bsz, :], w_d)
            for j in range(lo_in, hi_in):
                piece = part[(j - lo_in) * bsz:(j - lo_in + 1) * bsz, :]
                acc[j + d] = piece if acc[j + d] is None else acc[j + d] + piece
        return jnp.concatenate(acc, axis=1)

    f_row = (pl.program_id(0) % (SUBLANE // HY_CH)) * HY_CH
    for ch in range(HY_CH):
        for order in range(HY_ORDER):
            build_toeplitz(ch * HY_ORDER + order, f_ref[order, pl.ds(f_row + ch, 1), :])
    for ch in range(HY_CH):
        z0 = v_ref[ch]
        z1 = x1_ref[ch] * (long_conv(z0, ch * HY_ORDER) + bias_ref[0, c0 + ch] * z0)
        o_ref[ch] = x2_ref[ch] * (long_conv(z1, ch * HY_ORDER + 1) + bias_ref[1, c0 + ch] * z1)


def _hy_conv(uct, filt, bias):
    _, bsz, seq = uct.shape
    per_slab = HY_WIDTH // HY_CH
    chan = lambda k: pl.BlockSpec((HY_CH, bsz, seq), lambda c, k=k: (c + k * per_slab, 0, 0))
    return pl.pallas_call(
        _hy_conv_kernel,
        out_shape=jax.ShapeDtypeStruct((HY_WIDTH, bsz, seq), F32),
        grid=(per_slab,),
        in_specs=[pl.BlockSpec(memory_space=pltpu.SMEM), chan(0), chan(1), chan(2),
                  pl.BlockSpec((HY_ORDER, SUBLANE, 2 * seq), lambda c: (0, c * HY_CH // SUBLANE, 0))],
        out_specs=pl.BlockSpec((HY_CH, bsz, seq), lambda c: (c, 0, 0)),
        scratch_shapes=[pltpu.VMEM((HY_CH * HY_ORDER, HY_BLK, 2 * seq), BF16)],
        compiler_params=_cparams(("parallel",)), name="hy_conv")(bias, uct, uct, uct, filt)


def _swa_kernel(sink_ref, x_ref, o_ref, k_scr, vlo_scr, vhi_scr, bias_scr):
    seq = x_ref.shape[0]
    qb = SWA_QB
    nb = seq // qb
    win = qb + 2 * BLOCK
    grp = SWA_HEADS // SWA_KV_HEADS
    log2e = math.log2(math.e)
    scale = SWA_HEAD_DIM ** -0.5 * log2e
    kvw = SWA_KV_HEADS * LANE
    zeros = jnp.zeros((BLOCK, kvw), BF16)
    lane = lax.broadcasted_iota(jnp.int32, (1, LANE), 1)
    lo_half = lane < SWA_HEAD_DIM
    for scr in (k_scr, vlo_scr, vhi_scr):
        scr[0:BLOCK, :] = zeros
        scr[BLOCK + seq:2 * BLOCK + seq, :] = zeros
    body = slice(BLOCK, BLOCK + seq)
    k_both = x_ref[:, 256:256 + LANE]
    k_flip = pltpu.roll(k_both, SWA_HEAD_DIM, 1)
    k_scr[body, 0:LANE] = jnp.where(lo_half, k_both, k_flip).astype(BF16)
    k_scr[body, LANE:2 * LANE] = jnp.where(lo_half, k_flip, k_both).astype(BF16)
    v_both = x_ref[:, 256 + LANE:256 + 2 * LANE]
    v_flip = pltpu.roll(v_both, SWA_HEAD_DIM, 1)
    vlo_scr[body, 0:LANE] = jnp.where(lo_half, v_both, 0.0).astype(BF16)
    vhi_scr[body, 0:LANE] = jnp.where(lo_half, 0.0, v_flip).astype(BF16)
    vlo_scr[body, LANE:2 * LANE] = jnp.where(lo_half, v_flip, 0.0).astype(BF16)
    vhi_scr[body, LANE:2 * LANE] = jnp.where(lo_half, 0.0, v_both).astype(BF16)

    qi = lax.broadcasted_iota(jnp.int32, (grp * qb, win), 0)
    ji = lax.broadcasted_iota(jnp.int32, (grp * qb, win), 1)
    upper = qi >= qb
    dist = jnp.abs(jnp.where(upper, qi - qb, qi) + BLOCK - ji)
    slopes = [(2.0 ** (-8.0 / SWA_HEADS)) ** (i + 1) * log2e for i in range(SWA_HEADS)]
    for j in range(SWA_KV_HEADS):
        slope = jnp.where(upper, slopes[grp * j + 1], slopes[grp * j])
        bias_scr[j] = jnp.where(dist <= SWA_WINDOW, -slope * dist.astype(F32), NEG_INF)
    col = lax.broadcasted_iota(jnp.int32, (1, win), 1)
    row_up = lax.broadcasted_iota(jnp.int32, (grp * qb, 1), 0) >= qb

    def block(n, edge):
        r0 = n * qb if isinstance(n, int) else pl.multiple_of(n * qb, qb)
        for j in range(SWA_KV_HEADS):
            sl = slice(j * LANE, (j + 1) * LANE)
            qp = x_ref[pl.ds(r0, qb), sl] * scale
            q2 = jnp.concatenate([jnp.where(lo_half, qp, 0.0), jnp.where(lo_half, 0.0, qp)], 0).astype(BF16)
            sc = _dot_nt(q2, k_scr[pl.ds(r0, win), sl]) + bias_scr[j]
            if edge == "first":
                sc = jnp.where(col >= BLOCK, sc, NEG_INF)
            elif edge == "last":
                sc = jnp.where(col < qb + BLOCK, sc, NEG_INF)
            sk = jnp.where(row_up, sink_ref[grp * j + 1], sink_ref[grp * j]) * log2e
            m = jnp.maximum(jnp.max(sc, -1, keepdims=True), sk)
            e = jnp.exp2(sc - m)
            den = jnp.sum(e, -1, keepdims=True) + jnp.exp2(sk - m)
            eb = e.astype(BF16)
            o_lo = _dot(eb[0:qb], vlo_scr[pl.ds(r0, win), sl]) / den[0:qb]
            o_hi = _dot(eb[qb:], vhi_scr[pl.ds(r0, win), sl]) / den[qb:]
            o_ref[pl.ds(r0, qb), sl] = o_lo + o_hi

    def inner(n, carry):
        block(n, None)
        return carry

    block(0, "first")
    lax.fori_loop(1, nb - 1, inner, 0, unroll=2)
    block(nb - 1, "last")


def _swa(swa_in, sink):
    bsz, seq, _ = swa_in.shape
    kv_scr = pltpu.VMEM((seq + 2 * BLOCK, SWA_KV_HEADS * LANE), BF16)
    grp = SWA_HEADS // SWA_KV_HEADS
    return pl.pallas_call(
        _swa_kernel,
        out_shape=jax.ShapeDtypeStruct((bsz, seq, GROUP_WIDTH), F32),
        grid=(bsz,),
        in_specs=[pl.BlockSpec(memory_space=pltpu.SMEM),
                  pl.BlockSpec((None, seq, SWA_IN_W), lambda b: (b, 0, 0))],
        out_specs=pl.BlockSpec((None, seq, GROUP_WIDTH), lambda b: (b, 0, 0)),
        scratch_shapes=[kv_scr, kv_scr, kv_scr,
                        pltpu.VMEM((SWA_KV_HEADS, grp * SWA_QB, SWA_QB + 2 * BLOCK), F32)],
        compiler_params=_cparams(("parallel",)), name="swa")(sink, swa_in)


def _softplus(x):
    return jnp.maximum(x, 0.0) + jnp.log1p(jnp.exp(-jnp.abs(x)))


def _split3(x):
    hi = x.astype(BF16)
    rest = x - hi.astype(F32)
    mid = rest.astype(BF16)
    return hi, mid, (rest - mid.astype(F32)).astype(BF16)


def _ssd_kernel(x_ref, dtr_ref, cw_ref, cb_ref, dtb_col_ref, alog_col_ref, dskip_ref, o_ref,
                act_scr, yf_scr, yb_scr, rows_scr, cols_scr):
    seq = x_ref.shape[0]
    q = SSD_CHUNK
    nch = seq // q
    nh = 2 * SSD_HEADS
    hpg = SSD_HEADS // SSD_GROUPS
    row = lax.broadcasted_iota(jnp.int32, (seq, GROUP_WIDTH), 0)
    for k in range(3):
        sl = slice(k * GROUP_WIDTH, (k + 1) * GROUP_WIDTH)
        xin = x_ref[:, 256 + k * GROUP_WIDTH:256 + (k + 1) * GROUP_WIDTH]
        act_scr[:, sl] = _silu(_dwconv3(xin, cw_ref[:, sl], cb_ref[:, sl], row, seq))

    ti = lax.broadcasted_iota(jnp.int32, (q, q), 0)
    si = lax.broadcasted_iota(jnp.int32, (q, q), 1)
    lower = si <= ti
    upper = si >= ti
    tri_l = jnp.where(lower, 1.0, 0.0).astype(BF16)
    tri_u = jnp.where(upper, 1.0, 0.0).astype(BF16)
    lane = lax.broadcasted_iota(jnp.int32, (1, LANE), 1)
    first = lane < SSD_HEAD_DIM

    dt_all = _softplus(dtr_ref[...] + dtb_col_ref[...])
    da_parts = _split3(dt_all * -jnp.exp(alog_col_ref[...]))
    row_tri = jnp.concatenate([tri_u, tri_l], axis=1)
    col_tri = jnp.concatenate([tri_l, tri_u], axis=0)
    rows_scr[...] = sum(_dot(part, row_tri) for part in da_parts)
    cols_scr[0:2 * q, :] = sum(_dot_nt(col_tri, part) for part in da_parts)
    cols_scr[2 * q:3 * q, :] = dt_all.T

    def per_head(col, h0):
        return jnp.where(first, col[:, h0:h0 + 1], col[:, h0 + 1:h0 + 2])

    def one_chunk(direction, c, carry, y_scr):
        hb = direction * SSD_HEADS
        keep = lower if direction == 0 else upper
        edge = q - 1 if direction == 0 else 0
        rows = slice(c * q, (c + 1) * q)
        heads = slice(c * nh, (c + 1) * nh)
        cs_col = cols_scr[direction * q:(direction + 1) * q, heads]
        cs_row = rows_scr[heads, direction * q:(direction + 1) * q]
        dt_col = cols_scr[2 * q:3 * q, heads]
        new = []
        for g in range(SSD_GROUPS):
            gs = slice(g * LANE, (g + 1) * LANE)
            h0 = hb + g * hpg
            xg = act_scr[rows, gs]
            bg = act_scr[rows, GROUP_WIDTH + g * LANE:GROUP_WIDTH + (g + 1) * LANE]
            cg = act_scr[rows, 2 * GROUP_WIDTH + g * LANE:2 * GROUP_WIDTH + (g + 1) * LANE]
            xdt = xg * per_head(dt_col, h0)
            cb = _dot_nt(cg.astype(BF16), bg.astype(BF16))
            y = None
            for r in range(hpg):
                seg = cs_col[:, h0 + r:h0 + r + 1] - cs_row[h0 + r:h0 + r + 1, :]
                lmat = jnp.exp(jnp.where(keep, seg, NEG_INF))
                xh = jnp.where(first if r == 0 else ~first, xdt, 0.0)
                part = _dot((cb * lmat).astype(BF16), xh.astype(BF16))
                y = part if y is None else y + part
            cs_g = per_head(cs_col, h0)
            total = cs_g[edge:edge + 1, :]
            y_scr[rows, gs] = y + _dot(cg.astype(BF16), carry[g].astype(BF16)) * jnp.exp(cs_g)
            st = _dot(bg.T.astype(BF16), (xdt * jnp.exp(total - cs_g)).astype(BF16))
            new.append(carry[g] * jnp.exp(total) + st)
        return new

    fwd = [jnp.zeros((SSD_STATE, LANE), F32) for _ in range(SSD_GROUPS)]
    bwd = [jnp.zeros((SSD_STATE, LANE), F32) for _ in range(SSD_GROUPS)]
    for i in range(nch):
        fwd = one_chunk(0, i, fwd, yf_scr)
        bwd = one_chunk(1, nch - 1 - i, bwd, yb_scr)

    dsum = dskip_ref[0:1, :] + dskip_ref[1:2, :]
    y = yf_scr[...] + yb_scr[...] + dsum * act_scr[:, 0:GROUP_WIDTH]
    o_ref[...] = y * _silu(x_ref[:, 0:GROUP_WIDTH])


def _ssd(ssd_in, dt_rows, cw, cb, dtb_col, alog_col, dskip):
    bsz, seq, _ = ssd_in.shape
    q = SSD_CHUNK
    nch = seq // q
    nh = 2 * SSD_HEADS
    small = (cw, cb, dtb_col, alog_col, dskip)
    return pl.pallas_call(
        _ssd_kernel,
        out_shape=jax.ShapeDtypeStruct((bsz, seq, GROUP_WIDTH), F32),
        grid=(bsz,),
        in_specs=[pl.BlockSpec((None, seq, SSD_IN_W), lambda b: (b, 0, 0)),
                  pl.BlockSpec((None, nch * nh, q), lambda b: (b, 0, 0))]
                 + [_resident(a.shape) for a in small],
        out_specs=pl.BlockSpec((None, seq, GROUP_WIDTH), lambda b: (b, 0, 0)),
        scratch_shapes=[pltpu.VMEM((seq, 3 * GROUP_WIDTH), F32)] + [pltpu.VMEM((seq, GROUP_WIDTH), F32)] * 2
                       + [pltpu.VMEM((nch * nh, 2 * q), F32), pltpu.VMEM((3 * q, nch * nh), F32)],
        compiler_params=_cparams(("parallel",)), name="ssd")(ssd_in, dt_rows, *small)


def _tail_kernel(tiles_per_seq, *refs):
    y_refs, refs = refs[:12], refs[12:]
    (hp_ref, h_ref, hn_ref, p_ref, mg_ref, wo_ref, l1g_ref, l1b_ref, wg_ref, wu_ref, cw_ref, cb_ref, wd_ref,
     lg_ref, lb_ref, pwp_ref, pwg_ref, pbg_ref, plg_ref, plb_ref, o_ref) = refs
    i = pl.program_id(0)
    ext = ROW_TILE + 2 * FFN_HALO
    main = slice(FFN_HALO, FFN_HALO + ROW_TILE)
    parts = []
    for k in range(N_MIX):
        prev_ref, main_ref, next_ref = y_refs[3 * k:3 * k + 3]
        if k == HYENA_SLOT:
            rows = [prev_ref[...].T[LANE - FFN_HALO:, :], main_ref[...].T, next_ref[...].T[:FFN_HALO, :]]
        else:
            rows = [prev_ref[...], main_ref[...], next_ref[...]]
        y = jnp.concatenate(rows, axis=0)
        parts.append(_rms(y, mg_ref[:, k * GROUP_WIDTH:(k + 1) * GROUP_WIDTH]).astype(BF16))
    h_in = jnp.concatenate([hp_ref[...], h_ref[...], hn_ref[...]], axis=0)
    h_ext = _layer_norm(ALPHA * h_in + _dot(jnp.concatenate(parts, axis=1), wo_ref[...]),
                        l1g_ref[...], l1b_ref[...])
    hext = h_ext.astype(BF16)
    h = h_ext[main]
    hb = hext[main]
    row = lax.broadcasted_iota(jnp.int32, (ROW_TILE, 1), 0)
    keep_prev = jnp.where((row == 0) & (i % tiles_per_seq == 0), 0.0, 1.0)
    keep_next = jnp.where((row == ROW_TILE - 1) & (i % tiles_per_seq == tiles_per_seq - 1), 0.0, 1.0)
    f = None
    for c0 in range(0, D_FF, FFN_CHUNK):
        sl = slice(c0, c0 + FFN_CHUNK)
        gate = _dot(hext, wg_ref[:, sl])
        g_prev = pltpu.roll(gate, 1, 0)[main]
        g_next = pltpu.roll(gate, ext - 1, 0)[main]
        g_cur = gate[main]
        conv = (g_prev * keep_prev * cw_ref[0:1, sl] + g_cur * cw_ref[1:2, sl]
                + g_next * keep_next * cw_ref[2:3, sl] + cb_ref[:, sl])
        act = (_silu(conv) * _dot(hb, wu_ref[:, sl])).astype(BF16)
        part = _dot(act, wd_ref[sl, :])
        f = part if f is None else f + part
    h = _layer_norm(ALPHA * h + f, lg_ref[...], lb_ref[...])
    gate = _dot(h.astype(BF16), pwg_ref[...]) + pbg_ref[...]
    e = _dot(p_ref[...].astype(BF16), pwp_ref[...]) / (1.0 + jnp.exp(-gate))
    o_ref[...] = _layer_norm(ALPHA * h + e, plg_ref[...], plb_ref[...])


def _halo_specs(width, n):
    per_tile = ROW_TILE // FFN_HALO
    last = n // FFN_HALO - 1
    return [pl.BlockSpec((FFN_HALO, width), lambda i: (jnp.maximum(i * per_tile - 1, 0), 0)),
            pl.BlockSpec((ROW_TILE, width), lambda i: (i, 0)),
            pl.BlockSpec((FFN_HALO, width), lambda i: (jnp.minimum((i + 1) * per_tile, last), 0))]


def _halo_specs_channel_major(width, n):
    per_tile = ROW_TILE // LANE
    last = n // LANE - 1
    return [pl.BlockSpec((width, LANE), lambda i: (0, jnp.maximum(i * per_tile - 1, 0))),
            pl.BlockSpec((width, ROW_TILE), lambda i: (0, i)),
            pl.BlockSpec((width, LANE), lambda i: (0, jnp.minimum((i + 1) * per_tile, last)))]


def _tail(ys, h, p, layer, seq, small):
    n = h.shape[0]
    y_specs, y_args = [], []
    for k, y in enumerate(ys):
        y_specs += _halo_specs_channel_major(GROUP_WIDTH, n) if k == HYENA_SLOT else _halo_specs(GROUP_WIDTH, n)
        y_args += [y, y, y]
    return pl.pallas_call(
        functools.partial(_tail_kernel, seq // ROW_TILE),
        out_shape=jax.ShapeDtypeStruct((n, D_MODEL), F32),
        grid=(n // ROW_TILE,),
        in_specs=y_specs + _halo_specs(D_MODEL, n)
                 + [pl.BlockSpec((None, ROW_TILE, PLE_DIM), lambda i: (layer, i, 0))]
                 + [_resident(a.shape) for a in small],
        out_specs=pl.BlockSpec((ROW_TILE, D_MODEL), lambda i: (i, 0)),
        compiler_params=_cparams(("parallel",)), name="tail")(*y_args, h, h, h, p, *small)


def _pad_cols(w, width):
    return jnp.pad(w, ((0, 0), (0, width - w.shape[1])))


def _in_proj_weight(w):
    o = np.cumsum([0, 256, 128, 32, 768, 256, 128, 128, 256, 768, 8])
    cq, ckv, kr, hy, sq, sk, sv, z, xbc, dt = (w[:, o[i]:o[i + 1]] for i in range(10))
    zc = lambda n: jnp.zeros((w.shape[0], n), w.dtype)
    half = MLA_ROPE // 2
    misc = [dt, zc(KR_LANE - dt.shape[1]), kr, kr[:, :half], zc(MISC_W - KR_LANE - MLA_ROPE - half)]
    return jnp.concatenate([cq, ckv] + misc + [hy, sq, sk, sv, z, xbc], 1).astype(BF16)


def _mla_weights(w_uq, w_ukv):
    r = w_uq.shape[0]
    half = MLA_ROPE // 2
    qh = w_uq.reshape(r, MLA_HEADS, MLA_NOPE + MLA_ROPE)
    zq = jnp.zeros((r, MLA_HEADS, LANE - MLA_NOPE - MLA_ROPE - half), w_uq.dtype)
    wq = jnp.concatenate([qh, qh[..., MLA_NOPE:MLA_NOPE + half], zq], -1).reshape(r, MLA_HEADS * LANE)
    rk = w_ukv.shape[0]
    kvh = w_ukv.reshape(rk, MLA_HEADS, MLA_NOPE + MLA_V)
    zk = jnp.zeros((rk, MLA_HEADS, LANE - MLA_NOPE), w_ukv.dtype)
    wk = jnp.concatenate([kvh[..., :MLA_NOPE], zk], -1).reshape(rk, MLA_HEADS * LANE)
    v = kvh[..., MLA_NOPE:]
    zv = jnp.zeros_like(v)
    odd = (jnp.arange(MLA_HEADS) % 2 == 1)[None, :, None]
    wv = jnp.concatenate([jnp.where(odd, zv, v), jnp.where(odd, v, zv)], -1).reshape(rk, MLA_HEADS * LANE)
    return wq.astype(BF16), jnp.concatenate([wk, wv], 1).astype(BF16)


def _rope_tables(seq):
    f32 = np.float32
    inv_freq = (f32(ROPE_THETA) ** (-np.arange(0, MLA_ROPE, 2, dtype=f32) / f32(MLA_ROPE))).astype(f32)
    ang = np.arange(seq, dtype=f32)[:, None] * inv_freq[None, :]
    cos, sin = np.cos(ang), np.sin(ang)
    one = np.ones((seq, MLA_NOPE), f32)
    zero = lambda n: np.zeros((seq, n), f32)
    ct = np.concatenate([one, cos, cos, zero(LANE - MLA_NOPE - MLA_ROPE)], 1)
    st = np.concatenate([zero(MLA_NOPE), -sin, sin, zero(LANE - MLA_NOPE - MLA_ROPE)], 1)
    return jnp.asarray(ct), jnp.asarray(st)


def _hy_tables(seq):
    f32 = np.float32
    t = np.linspace(0.0, 1.0, seq, dtype=f32)[:, None]
    ang = (f32(2.0 * math.pi) * np.arange(seq, dtype=f32)[:, None] / f32(seq)).astype(f32)
    bands = np.linspace(1e-4, HY_BANDS - 1, HY_BANDS, dtype=f32)[None, :]
    feat = np.concatenate([t, np.cos(bands * ang), -np.sin(bands * ang)], -1).astype(f32)
    max_decay = math.log(HY_TARGET) / HY_FAST_DECAY
    min_decay = math.log(HY_TARGET) / HY_SLOW_DECAY
    deltas = np.linspace(min_decay, max_decay, HY_WIDTH, dtype=f32)
    decay = np.exp(-t * np.abs(deltas)[None, :]).astype(f32)
    m = np.arange(2 * seq)
    pos = np.where(m < seq, np.minimum(seq - m, seq - 1), m - seq)
    decay_lag = np.where((m == 0)[:, None], f32(0.0), decay[pos])
    feat_lag = np.pad(feat[pos], ((0, 0), (0, LANE - feat.shape[1])))
    return jnp.asarray(feat_lag), jnp.asarray(np.ascontiguousarray(decay_lag.T))


def _pad2(w, rows, cols):
    return jnp.pad(w, ((0, rows - w.shape[0]), (0, cols - w.shape[1])))


def _hy_tap_weights(w3):
    w = w3.reshape(w3.shape[0], HY_ORDER, 2, HY_WIDTH)
    rows = lambda d: _pad_cols(w[:, :, d].reshape(w3.shape[0], HY_ORDER * HY_WIDTH).T, LANE)
    return rows(0), rows(1)


def kernel(x, p, emb_ln_g, emb_ln_b, w_in, mla_q_norm, mla_kv_norm, mla_w_uq, mla_w_ukv, hy_conv_w, hy_conv_b, hy_f_w1, hy_f_b1, hy_f_freq, hy_f_w2, hy_f_b2, hy_f_w3, hy_bias, swa_sink, ssd_conv_w, ssd_conv_b, ssd_dt_bias, ssd_a_log, ssd_d, mix_norm_g, w_out, ln1_g, ln1_b, ffn_w_gate, ffn_w_up, ffn_conv_w, ffn_conv_b, ffn_w_down, ln2_g, ln2_b, ple_w_proj, ple_w_gate, ple_b_gate, ln3_g, ln3_b):
    bsz, seq, d = x.shape
    n = bsz * seq
    row = lambda v: v.reshape(1, -1)
    ct, st = _rope_tables(seq)
    feat, decay = _hy_tables(seq)
    h = x.reshape(n, d)
    for i in range(DEPTH):
        w_in_i = _in_proj_weight(w_in[i])
        if i == 0:
            h, mla_in, misc, hy_u, swa_in, ssd_in, dt_t = _in_proj(h, w_in_i, (row(emb_ln_g), row(emb_ln_b)))
        else:
            mla_in, misc, hy_u, swa_in, ssd_in, dt_t = _in_proj(h, w_in_i)

        wq, wkv = _mla_weights(mla_w_uq[i], mla_w_ukv[i])
        y_a = _mla(mla_in.reshape(bsz, seq, MLA_IN_W), misc.reshape(bsz, seq, MISC_W), ct, st,
                   row(mla_q_norm[i]), row(mla_kv_norm[i]), wq, wkv)

        w3f, w3b = _hy_tap_weights(hy_f_w3[i])
        filt = _hy_filter(feat, _pad2(hy_f_w1[i], LANE, LANE), _pad_cols(row(hy_f_b1[i]), LANE),
                          _pad_cols(row(hy_f_freq[i]), LANE), _pad2(hy_f_w2[i], LANE, LANE),
                          _pad_cols(row(hy_f_b2[i]), LANE), w3f, w3b, decay)
        uct = _hy_pre(hy_u.reshape(bsz, seq, HY_IN_W), hy_conv_w[i], row(hy_conv_b[i]))
        y_bt = _hy_conv(uct.reshape(HY_IN_W, bsz, seq), filt, hy_bias[i])

        y_c = _swa(swa_in.reshape(bsz, seq, SWA_IN_W), swa_sink[i])

        nch = seq // SSD_CHUNK
        dt_rows = jnp.transpose(dt_t.reshape(2 * SSD_HEADS, bsz, nch, SSD_CHUNK), (1, 2, 0, 3))
        dt_rows = dt_rows.reshape(bsz, nch * 2 * SSD_HEADS, SSD_CHUNK)
        per_row = lambda v: jnp.tile(v.reshape(-1, 1), (nch, 1))
        dskip = jnp.repeat(ssd_d[i], SSD_HEAD_DIM, axis=1)
        y_d = _ssd(ssd_in.reshape(bsz, seq, SSD_IN_W), dt_rows, ssd_conv_w[i], row(ssd_conv_b[i]),
                   per_row(ssd_dt_bias[i]), per_row(ssd_a_log[i]), dskip)

        flat = lambda y: y.reshape(n, GROUP_WIDTH)
        tail_params = (row(mix_norm_g[i]), w_out[i].astype(BF16), row(ln1_g[i]), row(ln1_b[i]),
                       ffn_w_gate[i].astype(BF16), ffn_w_up[i].astype(BF16), ffn_conv_w[i], row(ffn_conv_b[i]),
                       ffn_w_down[i].astype(BF16), row(ln2_g[i]), row(ln2_b[i]),
                       ple_w_proj[i].astype(BF16), ple_w_gate[i].astype(BF16), row(ple_b_gate[i]),
                       row(ln3_g[i]), row(ln3_b[i]))
        h = _tail((flat(y_a), y_bt.reshape(HY_WIDTH, n), flat(y_c), flat(y_d)), h,
                  p.reshape(DEPTH, n, PLE_DIM), i, seq, tail_params)
    return h.reshape(bsz, seq, d)
```

```python
import functools
import math

import jax
import jax.numpy as jnp
import numpy as np
from jax import lax
from jax.experimental import pallas as pl
from jax.experimental.pallas import tpu as pltpu

F32 = jnp.float32
BF16 = jnp.bfloat16

D_MODEL = 1024
DEPTH = 2
PLE_DIM = 256
GROUP_WIDTH = 256
N_MIX = 4
HYENA_SLOT = 1
BLOCK = 128

MLA_HEADS = 4
MLA_Q_RANK = 256
MLA_KV_RANK = 128
MLA_NOPE = 64
MLA_ROPE = 32
MLA_V = 64
ROPE_THETA = 10000.0

HY_WIDTH = 256
HY_ORDER = 2
HY_EMB = 33
HY_BANDS = 16
HY_FAST_DECAY = 0.3
HY_SLOW_DECAY = 1.5
HY_TARGET = 1e-2
HY_BLK = 256

SWA_HEADS = 4
SWA_KV_HEADS = 2
SWA_HEAD_DIM = 64
SWA_WINDOW = 128
SWA_QB = 256

SSD_D_INNER = 256
SSD_HEAD_DIM = 64
SSD_HEADS = 4
SSD_GROUPS = 2
SSD_STATE = 128
SSD_CHUNK = 128

D_FF = 2816
LN_EPS = 1e-5
RMS_EPS = 1e-6
NEG_INF = -1e30
ALPHA = (2.0 * DEPTH) ** 0.25

LANE = 128
SUBLANE = 8
VMEM_LIMIT = 56 * 1024 * 1024

MLA_IN_W = 384
MISC_W = 128
HY_IN_W = 768
SWA_IN_W = 512
SSD_IN_W = 1024
IN_WIDTHS = (MLA_IN_W, MISC_W, HY_IN_W, SWA_IN_W, SSD_IN_W)
KR_LANE = MLA_NOPE

ROW_TILE = 512
FFN_HALO = 16
FFN_CHUNK = 1408


def _cparams(sem):
    return pltpu.CompilerParams(dimension_semantics=sem, vmem_limit_bytes=VMEM_LIMIT)


def _resident(shape):
    nd = len(shape)
    return pl.BlockSpec(shape, lambda *_: (0,) * nd, pipeline_mode=pl.Buffered(1))


def _layer_norm(x, g, b):
    mu = jnp.mean(x, -1, keepdims=True)
    xc = x - mu
    var = jnp.mean(xc * xc, -1, keepdims=True)
    return xc * lax.rsqrt(var + LN_EPS) * g + b


def _rms(x, g):
    return x * lax.rsqrt(jnp.mean(x * x, -1, keepdims=True) + RMS_EPS) * g


def _silu(x):
    return x / (1.0 + jnp.exp(-x))


def _dot(a, b):
    return jnp.dot(a, b, preferred_element_type=F32)


def _dot_nt(a, b):
    return lax.dot_general(a, b, (((1,), (1,)), ((), ())), preferred_element_type=F32)


def _dot_hi(a, b):
    return jnp.dot(a, b, preferred_element_type=F32, precision=lax.Precision.HIGHEST)


def _dot_nt_hi(a, b):
    return lax.dot_general(a, b, (((1,), (1,)), ((), ())), preferred_element_type=F32,
                           precision=lax.Precision.HIGHEST)


def _in_proj_kernel(apply_ln, *refs):
    if apply_ln:
        x_ref, g_ref, b_ref, w_ref, h_ref = refs[:5]
        out_refs = refs[5:]
        h = _layer_norm(x_ref[...], g_ref[...], b_ref[...])
        h_ref[...] = h
    else:
        x_ref, w_ref = refs[:2]
        out_refs = refs[2:]
        h = x_ref[...]
    u = _dot(h.astype(BF16), w_ref[...])
    o = 0
    for ref, w in zip(out_refs, IN_WIDTHS):
        ref[...] = u[:, o:o + w]
        o += w
    out_refs[-1][...] = u[:, MLA_IN_W:MLA_IN_W + MISC_W].T[0:2 * SSD_HEADS, :]


def _in_proj(x, w, ln=None):
    n = x.shape[0]
    row = lambda wd: pl.BlockSpec((ROW_TILE, wd), lambda i: (i, 0))
    outs = [jax.ShapeDtypeStruct((n, wd), F32) for wd in IN_WIDTHS]
    outs.append(jax.ShapeDtypeStruct((2 * SSD_HEADS, n), F32))
    out_specs = [row(wd) for wd in IN_WIDTHS] + [pl.BlockSpec((2 * SSD_HEADS, ROW_TILE), lambda i: (0, i))]
    if ln is not None:
        args = (x, ln[0], ln[1], w)
        in_specs = [row(D_MODEL), _resident((1, D_MODEL)), _resident((1, D_MODEL)), _resident(w.shape)]
        outs = [jax.ShapeDtypeStruct((n, D_MODEL), F32)] + outs
        out_specs = [row(D_MODEL)] + out_specs
    else:
        args = (x, w)
        in_specs = [row(D_MODEL), _resident(w.shape)]
    return pl.pallas_call(
        functools.partial(_in_proj_kernel, ln is not None),
        out_shape=outs, grid=(n // ROW_TILE,), in_specs=in_specs, out_specs=out_specs,
        compiler_params=_cparams(("parallel",)), name="in_proj")(*args)


MLA_TQ = 512


def _mla_kernel(x_ref, misc_ref, ct_ref, st_ref, gq_ref, gkv_ref, wq_ref, wkv_ref, o_ref,
                q_scr, k_scr, v_scr, e_scr, l_scr):
    seq = x_ref.shape[0]
    scale = (MLA_NOPE + MLA_ROPE) ** -0.5 * math.log2(math.e)
    half = MLA_ROPE // 2
    ct = ct_ref[...]
    st = st_ref[...]
    cqn = _rms(x_ref[:, 0:256], gq_ref[...]).astype(BF16)
    ckvn = _rms(x_ref[:, 256:384], gkv_ref[...]).astype(BF16)
    lane = lax.broadcasted_iota(jnp.int32, (1, LANE), 1)
    on_rope = (lane >= KR_LANE) & (lane < KR_LANE + MLA_ROPE)

    def rotate(x):
        return x * ct + pltpu.roll(x, LANE - half, 1) * st

    k_rope = jnp.where(on_rope, rotate(misc_ref[...]), 0.0)
    for h in range(MLA_HEADS):
        sl = slice(h * LANE, (h + 1) * LANE)
        q_scr[:, sl] = (rotate(_dot(cqn, wq_ref[:, sl])) * scale).astype(BF16)
        k_scr[:, sl] = (_dot(ckvn, wkv_ref[:, sl]) + k_rope).astype(BF16)
        vsl = slice(MLA_HEADS * LANE + h * LANE, MLA_HEADS * LANE + (h + 1) * LANE)
        v_scr[:, sl] = _dot(ckvn, wkv_ref[:, vsl]).astype(BF16)

    def tile_rows(t):
        start = t * MLA_TQ if isinstance(t, int) else pl.multiple_of(t * MLA_TQ, MLA_TQ)
        return pl.ds(start, MLA_TQ)

    def scores(t, h):
        sl = slice(h * LANE, (h + 1) * LANE)
        s = _dot_nt(q_scr[tile_rows(t), sl], k_scr[:, sl])
        e = jnp.exp2(s - jnp.max(s, -1, keepdims=True))
        l_scr[h % 2] = jnp.sum(e, -1, keepdims=True)
        e_scr[h % 2] = e.astype(BF16)

    def values(t, h):
        sl = slice(h * LANE, (h + 1) * LANE)
        o = _dot(e_scr[h % 2], v_scr[:, sl]) / l_scr[h % 2]
        out = (tile_rows(t), slice((h // 2) * LANE, (h // 2 + 1) * LANE))
        if h % 2 == 0:
            o_ref[out] = o
        else:
            o_ref[out] += o

    def q_tile(t, last):
        for h in range(MLA_HEADS):
            values(t, h)
            if h + 1 < MLA_HEADS:
                scores(t, h + 1)
            elif not last:
                scores(t + 1, 0)

    def body(t, carry):
        q_tile(t, False)
        return carry

    n_tiles = seq // MLA_TQ
    scores(0, 0)
    lax.fori_loop(0, n_tiles - 1, body, 0)
    q_tile(n_tiles - 1, True)


def _mla(mla_in, misc, ct, st, gq, gkv, wq, wkv):
    bsz, seq, _ = mla_in.shape
    return pl.pallas_call(
        _mla_kernel,
        out_shape=jax.ShapeDtypeStruct((bsz, seq, GROUP_WIDTH), F32),
        grid=(bsz,),
        in_specs=[pl.BlockSpec((None, seq, MLA_IN_W), lambda b: (b, 0, 0)),
                  pl.BlockSpec((None, seq, MISC_W), lambda b: (b, 0, 0)),
                  _resident(ct.shape), _resident(st.shape), _resident(gq.shape), _resident(gkv.shape),
                  _resident(wq.shape), _resident(wkv.shape)],
        out_specs=pl.BlockSpec((None, seq, GROUP_WIDTH), lambda b: (b, 0, 0)),
        scratch_shapes=[pltpu.VMEM((seq, MLA_HEADS * LANE), BF16)] * 3
                       + [pltpu.VMEM((2, MLA_TQ, seq), BF16), pltpu.VMEM((2, MLA_TQ, 1), F32)],
        compiler_params=_cparams(("parallel",)), name="mla")(mla_in, misc, ct, st, gq, gkv, wq, wkv)


def _hy_filter_kernel(feat_ref, w1_ref, b1_ref, fr_ref, w2_ref, b2_ref, w3f_ref, w3b_ref, decay_ref, o_ref):
    seq = feat_ref.shape[0] // 2
    fr = fr_ref[...]
    h = jnp.sin(fr * (_dot_hi(feat_ref[...], w1_ref[...]) + b1_ref[...]))
    h = jnp.sin(fr * (_dot_hi(h, w2_ref[...]) + b2_ref[...]))
    for o in range(HY_ORDER):
        rows = slice(o * HY_WIDTH, (o + 1) * HY_WIDTH)
        o_ref[o, :, 0:seq] = _dot_nt_hi(w3b_ref[rows, :], h[0:seq]) * decay_ref[:, 0:seq]
        o_ref[o, :, seq:2 * seq] = _dot_nt_hi(w3f_ref[rows, :], h[seq:2 * seq]) * decay_ref[:, seq:2 * seq]


def _hy_filter(feat, w1, b1, fr, w2, b2, w3f, w3b, decay):
    return pl.pallas_call(
        _hy_filter_kernel,
        out_shape=jax.ShapeDtypeStruct((HY_ORDER, HY_WIDTH, feat.shape[0]), F32),
        compiler_params=pltpu.CompilerParams(vmem_limit_bytes=VMEM_LIMIT),
        name="hy_filter")(feat, w1, b1, fr, w2, b2, w3f, w3b, decay)


def _shift_rows(x, row, seq):
    prev = jnp.where(row == 0, 0.0, pltpu.roll(x, 1, 0))
    nxt = jnp.where(row == seq - 1, 0.0, pltpu.roll(x, seq - 1, 0))
    return prev, nxt


def _dwconv3(x, w, b, row, seq):
    prev, nxt = _shift_rows(x, row, seq)
    return prev * w[0:1] + x * w[1:2] + nxt * w[2:3] + b


def _hy_pre_kernel(u_ref, w_ref, b_ref, o_ref):
    seq = u_ref.shape[0]
    row = lax.broadcasted_iota(jnp.int32, (seq, HY_WIDTH), 0)
    for k in range(HY_ORDER + 1):
        sl = slice(k * HY_WIDTH, (k + 1) * HY_WIDTH)
        o_ref[sl, :] = _dwconv3(u_ref[:, sl], w_ref[:, sl], b_ref[:, sl], row, seq).T


def _hy_pre(hy_u, w, b):
    bsz, seq, wd = hy_u.shape
    return pl.pallas_call(
        _hy_pre_kernel,
        out_shape=jax.ShapeDtypeStruct((wd, bsz * seq), F32),
        grid=(bsz,),
        in_specs=[pl.BlockSpec((None, seq, wd), lambda b: (b, 0, 0)), _resident(w.shape), _resident(b.shape)],
        out_specs=pl.BlockSpec((wd, seq), lambda b: (0, b)),
        compiler_params=_cparams(("parallel",)), name="hy_pre")(hy_u, w, b)


HY_ROWS16 = 16
HY_CH = 4


def _hy_conv_kernel(bias_ref, v_ref, x1_ref, x2_ref, f_ref, o_ref, r_scr):
    _, bsz, seq = v_ref.shape
    nblk = seq // HY_BLK
    c0 = pl.program_id(0) * HY_CH

    def build_toeplitz(slot, taps):
        k16 = jnp.broadcast_to(taps, (HY_ROWS16, 2 * seq))
        k16 = pltpu.roll(k16, 0, 1, stride=1, stride_axis=0)
        for g in range(LANE // HY_ROWS16):
            rows = slice(g * HY_ROWS16, (g + 1) * HY_ROWS16)
            r_scr[slot, rows, :] = (k16 if g == 0 else pltpu.roll(k16, g * HY_ROWS16, 1)).astype(BF16)
        r_scr[slot, LANE:2 * LANE, LANE:] = r_scr[slot, 0:LANE, 0:2 * seq - LANE]

    def long_conv(z, slot):
        zr = jnp.concatenate([z[:, j * HY_BLK:(j + 1) * HY_BLK] for j in range(nblk)], axis=0).astype(BF16)
        acc = [None] * nblk
        for d in range(-(nblk - 1), nblk):
            lo_in, hi_in = max(0, -d), nblk - max(0, d)
            w_d = r_scr[slot, :, seq + d * HY_BLK: seq + (d + 1) * HY_BLK]
            part = _dot(zr[lo_in * bsz:hi_in * bsz, :], w_d)
            for j in range(lo_in, hi_in):
                piece = part[(j - lo_in) * bsz:(j - lo_in + 1) * bsz, :]
                acc[j + d] = piece if acc[j + d] is None else acc[j + d] + piece
        return jnp.concatenate(acc, axis=1)

    f_row = (pl.program_id(0) % (SUBLANE // HY_CH)) * HY_CH
    for ch in range(HY_CH):
        for order in range(HY_ORDER):
            build_toeplitz(ch * HY_ORDER + order, f_ref[order, pl.ds(f_row + ch, 1), :])
    for ch in range(HY_CH):
        z0 = v_ref[ch]
        z1 = x1_ref[ch] * (long_conv(z0, ch * HY_ORDER) + bias_ref[0, c0 + ch] * z0)
        o_ref[ch] = x2_ref[ch] * (long_conv(z1, ch * HY_ORDER + 1) + bias_ref[1, c0 + ch] * z1)


def _hy_conv(uct, filt, bias):
    _, bsz, seq = uct.shape
    per_slab = HY_WIDTH // HY_CH
    chan = lambda k: pl.BlockSpec((HY_CH, bsz, seq), lambda c, k=k: (c + k * per_slab, 0, 0))
    return pl.pallas_call(
        _hy_conv_kernel,
        out_shape=jax.ShapeDtypeStruct((HY_WIDTH, bsz, seq), F32),
        grid=(per_slab,),
        in_specs=[pl.BlockSpec(memory_space=pltpu.SMEM), chan(0), chan(1), chan(2),
                  pl.BlockSpec((HY_ORDER, SUBLANE, 2 * seq), lambda c: (0, c * HY_CH // SUBLANE, 0))],
        out_specs=pl.BlockSpec((HY_CH, bsz, seq), lambda c: (c, 0, 0)),
        scratch_shapes=[pltpu.VMEM((HY_CH * HY_ORDER, HY_BLK, 2 * seq), BF16)],
        compiler_params=_cparams(("parallel",)), name="hy_conv")(bias, uct, uct, uct, filt)


def _swa_kernel(sink_ref, x_ref, o_ref, k_scr, vlo_scr, vhi_scr, bias_scr):
    seq = x_ref.shape[0]
    qb = SWA_QB
    nb = seq // qb
    win = qb + 2 * BLOCK
    grp = SWA_HEADS // SWA_KV_HEADS
    log2e = math.log2(math.e)
    scale = SWA_HEAD_DIM ** -0.5 * log2e
    kvw = SWA_KV_HEADS * LANE
    zeros = jnp.zeros((BLOCK, kvw), BF16)
    lane = lax.broadcasted_iota(jnp.int32, (1, LANE), 1)
    lo_half = lane < SWA_HEAD_DIM
    for scr in (k_scr, vlo_scr, vhi_scr):
        scr[0:BLOCK, :] = zeros
        scr[BLOCK + seq:2 * BLOCK + seq, :] = zeros
    body = slice(BLOCK, BLOCK + seq)
    k_both = x_ref[:, 256:256 + LANE]
    k_flip = pltpu.roll(k_both, SWA_HEAD_DIM, 1)
    k_scr[body, 0:LANE] = jnp.where(lo_half, k_both, k_flip).astype(BF16)
    k_scr[body, LANE:2 * LANE] = jnp.where(lo_half, k_flip, k_both).astype(BF16)
    v_both = x_ref[:, 256 + LANE:256 + 2 * LANE]
    v_flip = pltpu.roll(v_both, SWA_HEAD_DIM, 1)
    vlo_scr[body, 0:LANE] = jnp.where(lo_half, v_both, 0.0).astype(BF16)
    vhi_scr[body, 0:LANE] = jnp.where(lo_half, 0.0, v_flip).astype(BF16)
    vlo_scr[body, LANE:2 * LANE] = jnp.where(lo_half, v_flip, 0.0).astype(BF16)
    vhi_scr[body, LANE:2 * LANE] = jnp.where(lo_half, 0.0, v_both).astype(BF16)

    qi = lax.broadcasted_iota(jnp.int32, (grp * qb, win), 0)
    ji = lax.broadcasted_iota(jnp.int32, (grp * qb, win), 1)
    upper = qi >= qb
    dist = jnp.abs(jnp.where(upper, qi - qb, qi) + BLOCK - ji)
    slopes = [(2.0 ** (-8.0 / SWA_HEADS)) ** (i + 1) * log2e for i in range(SWA_HEADS)]
    for j in range(SWA_KV_HEADS):
        slope = jnp.where(upper, slopes[grp * j + 1], slopes[grp * j])
        bias_scr[j] = jnp.where(dist <= SWA_WINDOW, -slope * dist.astype(F32), NEG_INF)
    col = lax.broadcasted_iota(jnp.int32, (1, win), 1)
    row_up = lax.broadcasted_iota(jnp.int32, (grp * qb, 1), 0) >= qb

    def block(n, edge):
        r0 = n * qb if isinstance(n, int) else pl.multiple_of(n * qb, qb)
        for j in range(SWA_KV_HEADS):
            sl = slice(j * LANE, (j + 1) * LANE)
            qp = x_ref[pl.ds(r0, qb), sl] * scale
            q2 = jnp.concatenate([jnp.where(lo_half, qp, 0.0), jnp.where(lo_half, 0.0, qp)], 0).astype(BF16)
            sc = _dot_nt(q2, k_scr[pl.ds(r0, win), sl]) + bias_scr[j]
            if edge == "first":
                sc = jnp.where(col >= BLOCK, sc, NEG_INF)
            elif edge == "last":
                sc = jnp.where(col < qb + BLOCK, sc, NEG_INF)
            sk = jnp.where(row_up, sink_ref[grp * j + 1], sink_ref[grp * j]) * log2e
            m = jnp.maximum(jnp.max(sc, -1, keepdims=True), sk)
            e = jnp.exp2(sc - m)
            den = jnp.sum(e, -1, keepdims=True) + jnp.exp2(sk - m)
            eb = e.astype(BF16)
            o_lo = _dot(eb[0:qb], vlo_scr[pl.ds(r0, win), sl]) / den[0:qb]
            o_hi = _dot(eb[qb:], vhi_scr[pl.ds(r0, win), sl]) / den[qb:]
            o_ref[pl.ds(r0, qb), sl] = o_lo + o_hi

    def inner(n, carry):
        block(n, None)
        return carry

    block(0, "first")
    lax.fori_loop(1, nb - 1, inner, 0, unroll=2)
    block(nb - 1, "last")


def _swa(swa_in, sink):
    bsz, seq, _ = swa_in.shape
    kv_scr = pltpu.VMEM((seq + 2 * BLOCK, SWA_KV_HEADS * LANE), BF16)
    grp = SWA_HEADS // SWA_KV_HEADS
    return pl.pallas_call(
        _swa_kernel,
        out_shape=jax.ShapeDtypeStruct((bsz, seq, GROUP_WIDTH), F32),
        grid=(bsz,),
        in_specs=[pl.BlockSpec(memory_space=pltpu.SMEM),
                  pl.BlockSpec((None, seq, SWA_IN_W), lambda b: (b, 0, 0))],
        out_specs=pl.BlockSpec((None, seq, GROUP_WIDTH), lambda b: (b, 0, 0)),
        scratch_shapes=[kv_scr, kv_scr, kv_scr,
                        pltpu.VMEM((SWA_KV_HEADS, grp * SWA_QB, SWA_QB + 2 * BLOCK), F32)],
        compiler_params=_cparams(("parallel",)), name="swa")(sink, swa_in)


def _softplus(x):
    return jnp.maximum(x, 0.0) + jnp.log1p(jnp.exp(-jnp.abs(x)))


def _split3(x):
    hi = x.astype(BF16)
    rest = x - hi.astype(F32)
    mid = rest.astype(BF16)
    return hi, mid, (rest - mid.astype(F32)).astype(BF16)


def _ssd_kernel(x_ref, dtr_ref, cw_ref, cb_ref, dtb_col_ref, alog_col_ref, dskip_ref, o_ref,
                act_scr, yf_scr, yb_scr, rows_scr, cols_scr):
    seq = x_ref.shape[0]
    q = SSD_CHUNK
    nch = seq // q
    nh = 2 * SSD_HEADS
    hpg = SSD_HEADS // SSD_GROUPS
    row = lax.broadcasted_iota(jnp.int32, (seq, GROUP_WIDTH), 0)
    for k in range(3):
        sl = slice(k * GROUP_WIDTH, (k + 1) * GROUP_WIDTH)
        xin = x_ref[:, 256 + k * GROUP_WIDTH:256 + (k + 1) * GROUP_WIDTH]
        act_scr[:, sl] = _silu(_dwconv3(xin, cw_ref[:, sl], cb_ref[:, sl], row, seq))

    ti = lax.broadcasted_iota(jnp.int32, (q, q), 0)
    si = lax.broadcasted_iota(jnp.int32, (q, q), 1)
    lower = si <= ti
    upper = si >= ti
    tri_l = jnp.where(lower, 1.0, 0.0).astype(BF16)
    tri_u = jnp.where(upper, 1.0, 0.0).astype(BF16)
    lane = lax.broadcasted_iota(jnp.int32, (1, LANE), 1)
    first = lane < SSD_HEAD_DIM

    dt_all = _softplus(dtr_ref[...] + dtb_col_ref[...])
    da_parts = _split3(dt_all * -jnp.exp(alog_col_ref[...]))
    row_tri = jnp.concatenate([tri_u, tri_l], axis=1)
    col_tri = jnp.concatenate([tri_l, tri_u], axis=0)
    rows_scr[...] = sum(_dot(part, row_tri) for part in da_parts)
    cols_scr[0:2 * q, :] = sum(_dot_nt(col_tri, part) for part in da_parts)
    cols_scr[2 * q:3 * q, :] = dt_all.T

    def per_head(col, h0):
        return jnp.where(first, col[:, h0:h0 + 1], col[:, h0 + 1:h0 + 2])

    def one_chunk(direction, c, carry, y_scr):
        hb = direction * SSD_HEADS
        keep = lower if direction == 0 else upper
        edge = q - 1 if direction == 0 else 0
        rows = slice(c * q, (c + 1) * q)
        heads = slice(c * nh, (c + 1) * nh)
        cs_col = cols_scr[direction * q:(direction + 1) * q, heads]
        cs_row = rows_scr[heads, direction * q:(direction + 1) * q]
        dt_col = cols_scr[2 * q:3 * q, heads]
        new = []
        for g in range(SSD_GROUPS):
            gs = slice(g * LANE, (g + 1) * LANE)
            h0 = hb + g * hpg
            xg = act_scr[rows, gs]
            bg = act_scr[rows, GROUP_WIDTH + g * LANE:GROUP_WIDTH + (g + 1) * LANE]
            cg = act_scr[rows, 2 * GROUP_WIDTH + g * LANE:2 * GROUP_WIDTH + (g + 1) * LANE]
            xdt = xg * per_head(dt_col, h0)
            cb = _dot_nt(cg.astype(BF16), bg.astype(BF16))
            y = None
            for r in range(hpg):
                seg = cs_col[:, h0 + r:h0 + r + 1] - cs_row[h0 + r:h0 + r + 1, :]
                lmat = jnp.exp(jnp.where(keep, seg, NEG_INF))
                xh = jnp.where(first if r == 0 else ~first, xdt, 0.0)
                part = _dot((cb * lmat).astype(BF16), xh.astype(BF16))
                y = part if y is None else y + part
            cs_g = per_head(cs_col, h0)
            total = cs_g[edge:edge + 1, :]
            y_scr[rows, gs] = y + _dot(cg.astype(BF16), carry[g].astype(BF16)) * jnp.exp(cs_g)
            st = _dot(bg.T.astype(BF16), (xdt * jnp.exp(total - cs_g)).astype(BF16))
            new.append(carry[g] * jnp.exp(total) + st)
        return new

    fwd = [jnp.zeros((SSD_STATE, LANE), F32) for _ in range(SSD_GROUPS)]
    bwd = [jnp.zeros((SSD_STATE, LANE), F32) for _ in range(SSD_GROUPS)]
    for i in range(nch):
        fwd = one_chunk(0, i, fwd, yf_scr)
        bwd = one_chunk(1, nch - 1 - i, bwd, yb_scr)

    dsum = dskip_ref[0:1, :] + dskip_ref[1:2, :]
    y = yf_scr[...] + yb_scr[...] + dsum * act_scr[:, 0:GROUP_WIDTH]
    o_ref[...] = y * _silu(x_ref[:, 0:GROUP_WIDTH])


def _ssd(ssd_in, dt_rows, cw, cb, dtb_col, alog_col, dskip):
    bsz, seq, _ = ssd_in.shape
    q = SSD_CHUNK
    nch = seq // q
    nh = 2 * SSD_HEADS
    small = (cw, cb, dtb_col, alog_col, dskip)
    return pl.pallas_call(
        _ssd_kernel,
        out_shape=jax.ShapeDtypeStruct((bsz, seq, GROUP_WIDTH), F32),
        grid=(bsz,),
        in_specs=[pl.BlockSpec((None, seq, SSD_IN_W), lambda b: (b, 0, 0)),
                  pl.BlockSpec((None, nch * nh, q), lambda b: (b, 0, 0))]
                 + [_resident(a.shape) for a in small],
        out_specs=pl.BlockSpec((None, seq, GROUP_WIDTH), lambda b: (b, 0, 0)),
        scratch_shapes=[pltpu.VMEM((seq, 3 * GROUP_WIDTH), F32)] + [pltpu.VMEM((seq, GROUP_WIDTH), F32)] * 2
                       + [pltpu.VMEM((nch * nh, 2 * q), F32), pltpu.VMEM((3 * q, nch * nh), F32)],
        compiler_params=_cparams(("parallel",)), name="ssd")(ssd_in, dt_rows, *small)


def _tail_kernel(tiles_per_seq, *refs):
    y_refs, refs = refs[:12], refs[12:]
    (hp_ref, h_ref, hn_ref, p_ref, mg_ref, wo_ref, l1g_ref, l1b_ref, wg_ref, wu_ref, cw_ref, cb_ref, wd_ref,
     lg_ref, lb_ref, pwp_ref, pwg_ref, pbg_ref, plg_ref, plb_ref, o_ref) = refs
    i = pl.program_id(0)
    ext = ROW_TILE + 2 * FFN_HALO
    main = slice(FFN_HALO, FFN_HALO + ROW_TILE)
    parts = []
    for k in range(N_MIX):
        prev_ref, main_ref, next_ref = y_refs[3 * k:3 * k + 3]
        if k == HYENA_SLOT:
            rows = [prev_ref[...].T[LANE - FFN_HALO:, :], main_ref[...].T, next_ref[...].T[:FFN_HALO, :]]
        else:
            rows = [prev_ref[...], main_ref[...], next_ref[...]]
        y = jnp.concatenate(rows, axis=0)
        parts.append(_rms(y, mg_ref[:, k * GROUP_WIDTH:(k + 1) * GROUP_WIDTH]).astype(BF16))
    h_in = jnp.concatenate([hp_ref[...], h_ref[...], hn_ref[...]], axis=0)
    h_ext = _layer_norm(ALPHA * h_in + _dot(jnp.concatenate(parts, axis=1), wo_ref[...]),
                        l1g_ref[...], l1b_ref[...])
    hext = h_ext.astype(BF16)
    h = h_ext[main]
    hb = hext[main]
    row = lax.broadcasted_iota(jnp.int32, (ROW_TILE, 1), 0)
    keep_prev = jnp.where((row == 0) & (i % tiles_per_seq == 0), 0.0, 1.0)
    keep_next = jnp.where((row == ROW_TILE - 1) & (i % tiles_per_seq == tiles_per_seq - 1), 0.0, 1.0)
    f = None
    for c0 in range(0, D_FF, FFN_CHUNK):
        sl = slice(c0, c0 + FFN_CHUNK)
        gate = _dot(hext, wg_ref[:, sl])
        g_prev = pltpu.roll(gate, 1, 0)[main]
        g_next = pltpu.roll(gate, ext - 1, 0)[main]
        g_cur = gate[main]
        conv = (g_prev * keep_prev * cw_ref[0:1, sl] + g_cur * cw_ref[1:2, sl]
                + g_next * keep_next * cw_ref[2:3, sl] + cb_ref[:, sl])
        act = (_silu(conv) * _dot(hb, wu_ref[:, sl])).astype(BF16)
        part = _dot(act, wd_ref[sl, :])
        f = part if f is None else f + part
    h = _layer_norm(ALPHA * h + f, lg_ref[...], lb_ref[...])
    gate = _dot(h.astype(BF16), pwg_ref[...]) + pbg_ref[...]
    e = _dot(p_ref[...].astype(BF16), pwp_ref[...]) / (1.0 + jnp.exp(-gate))
    o_ref[...] = _layer_norm(ALPHA * h + e, plg_ref[...], plb_ref[...])


def _halo_specs(width, n):
    per_tile = ROW_TILE // FFN_HALO
    last = n // FFN_HALO - 1
    return [pl.BlockSpec((FFN_HALO, width), lambda i: (jnp.maximum(i * per_tile - 1, 0), 0)),
            pl.BlockSpec((ROW_TILE, width), lambda i: (i, 0)),
            pl.BlockSpec((FFN_HALO, width), lambda i: (jnp.minimum((i + 1) * per_tile, last), 0))]


def _halo_specs_channel_major(width, n):
    per_tile = ROW_TILE // LANE
    last = n // LANE - 1
    return [pl.BlockSpec((width, LANE), lambda i: (0, jnp.maximum(i * per_tile - 1, 0))),
            pl.BlockSpec((width, ROW_TILE), lambda i: (0, i)),
            pl.BlockSpec((width, LANE), lambda i: (0, jnp.minimum((i + 1) * per_tile, last)))]


def _tail(ys, h, p, layer, seq, small):
    n = h.shape[0]
    y_specs, y_args = [], []
    for k, y in enumerate(ys):
        y_specs += _halo_specs_channel_major(GROUP_WIDTH, n) if k == HYENA_SLOT else _halo_specs(GROUP_WIDTH, n)
        y_args += [y, y, y]
    return pl.pallas_call(
        functools.partial(_tail_kernel, seq // ROW_TILE),
        out_shape=jax.ShapeDtypeStruct((n, D_MODEL), F32),
        grid=(n // ROW_TILE,),
        in_specs=y_specs + _halo_specs(D_MODEL, n)
                 + [pl.BlockSpec((None, ROW_TILE, PLE_DIM), lambda i: (layer, i, 0))]
                 + [_resident(a.shape) for a in small],
        out_specs=pl.BlockSpec((ROW_TILE, D_MODEL), lambda i: (i, 0)),
        compiler_params=_cparams(("parallel",)), name="tail")(*y_args, h, h, h, p, *small)


def _pad_cols(w, width):
    return jnp.pad(w, ((0, 0), (0, width - w.shape[1])))


def _in_proj_weight(w):
    o = np.cumsum([0, 256, 128, 32, 768, 256, 128, 128, 256, 768, 8])
    cq, ckv, kr, hy, sq, sk, sv, z, xbc, dt = (w[:, o[i]:o[i + 1]] for i in range(10))
    zc = lambda n: jnp.zeros((w.shape[0], n), w.dtype)
    half = MLA_ROPE // 2
    misc = [dt, zc(KR_LANE - dt.shape[1]), kr, kr[:, :half], zc(MISC_W - KR_LANE - MLA_ROPE - half)]
    return jnp.concatenate([cq, ckv] + misc + [hy, sq, sk, sv, z, xbc], 1).astype(BF16)


def _mla_weights(w_uq, w_ukv):
    r = w_uq.shape[0]
    half = MLA_ROPE // 2
    qh = w_uq.reshape(r, MLA_HEADS, MLA_NOPE + MLA_ROPE)
    zq = jnp.zeros((r, MLA_HEADS, LANE - MLA_NOPE - MLA_ROPE - half), w_uq.dtype)
    wq = jnp.concatenate([qh, qh[..., MLA_NOPE:MLA_NOPE + half], zq], -1).reshape(r, MLA_HEADS * LANE)
    rk = w_ukv.shape[0]
    kvh = w_ukv.reshape(rk, MLA_HEADS, MLA_NOPE + MLA_V)
    zk = jnp.zeros((rk, MLA_HEADS, LANE - MLA_NOPE), w_ukv.dtype)
    wk = jnp.concatenate([kvh[..., :MLA_NOPE], zk], -1).reshape(rk, MLA_HEADS * LANE)
    v = kvh[..., MLA_NOPE:]
    zv = jnp.zeros_like(v)
    odd = (jnp.arange(MLA_HEADS) % 2 == 1)[None, :, None]
    wv = jnp.concatenate([jnp.where(odd, zv, v), jnp.where(odd, v, zv)], -1).reshape(rk, MLA_HEADS * LANE)
    return wq.astype(BF16), jnp.concatenate([wk, wv], 1).astype(BF16)


def _rope_tables(seq):
    f32 = np.float32
    inv_freq = (f32(ROPE_THETA) ** (-np.arange(0, MLA_ROPE, 2, dtype=f32) / f32(MLA_ROPE))).astype(f32)
    ang = np.arange(seq, dtype=f32)[:, None] * inv_freq[None, :]
    cos, sin = np.cos(ang), np.sin(ang)
    one = np.ones((seq, MLA_NOPE), f32)
    zero = lambda n: np.zeros((seq, n), f32)
    ct = np.concatenate([one, cos, cos, zero(LANE - MLA_NOPE - MLA_ROPE)], 1)
    st = np.concatenate([zero(MLA_NOPE), -sin, sin, zero(LANE - MLA_NOPE - MLA_ROPE)], 1)
    return jnp.asarray(ct), jnp.asarray(st)


def _hy_tables(seq):
    f32 = np.float32
    t = np.linspace(0.0, 1.0, seq, dtype=f32)[:, None]
    ang = (f32(2.0 * math.pi) * np.arange(seq, dtype=f32)[:, None] / f32(seq)).astype(f32)
    bands = np.linspace(1e-4, HY_BANDS - 1, HY_BANDS, dtype=f32)[None, :]
    feat = np.concatenate([t, np.cos(bands * ang), -np.sin(bands * ang)], -1).astype(f32)
    max_decay = math.log(HY_TARGET) / HY_FAST_DECAY
    min_decay = math.log(HY_TARGET) / HY_SLOW_DECAY
    deltas = np.linspace(min_decay, max_decay, HY_WIDTH, dtype=f32)
    decay = np.exp(-t * np.abs(deltas)[None, :]).astype(f32)
    m = np.arange(2 * seq)
    pos = np.where(m < seq, np.minimum(seq - m, seq - 1), m - seq)
    decay_lag = np.where((m == 0)[:, None], f32(0.0), decay[pos])
    feat_lag = np.pad(feat[pos], ((0, 0), (0, LANE - feat.shape[1])))
    return jnp.asarray(feat_lag), jnp.asarray(np.ascontiguousarray(decay_lag.T))


def _pad2(w, rows, cols):
    return jnp.pad(w, ((0, rows - w.shape[0]), (0, cols - w.shape[1])))


def _hy_tap_weights(w3):
    w = w3.reshape(w3.shape[0], HY_ORDER, 2, HY_WIDTH)
    rows = lambda d: _pad_cols(w[:, :, d].reshape(w3.shape[0], HY_ORDER * HY_WIDTH).T, LANE)
    return rows(0), rows(1)


def kernel(x, p, emb_ln_g, emb_ln_b, w_in, mla_q_norm, mla_kv_norm, mla_w_uq, mla_w_ukv, hy_conv_w, hy_conv_b, hy_f_w1, hy_f_b1, hy_f_freq, hy_f_w2, hy_f_b2, hy_f_w3, hy_bias, swa_sink, ssd_conv_w, ssd_conv_b, ssd_dt_bias, ssd_a_log, ssd_d, mix_norm_g, w_out, ln1_g, ln1_b, ffn_w_gate, ffn_w_up, ffn_conv_w, ffn_conv_b, ffn_w_down, ln2_g, ln2_b, ple_w_proj, ple_w_gate, ple_b_gate, ln3_g, ln3_b):
    bsz, seq, d = x.shape
    n = bsz * seq
    row = lambda v: v.reshape(1, -1)
    ct, st = _rope_tables(seq)
    feat, decay = _hy_tables(seq)
    h = x.reshape(n, d)
    for i in range(DEPTH):
        w_in_i = _in_proj_weight(w_in[i])
        if i == 0:
            h, mla_in, misc, hy_u, swa_in, ssd_in, dt_t = _in_proj(h, w_in_i, (row(emb_ln_g), row(emb_ln_b)))
        else:
            mla_in, misc, hy_u, swa_in, ssd_in, dt_t = _in_proj(h, w_in_i)

        uct = _hy_pre(hy_u.reshape(bsz, seq, HY_IN_W), hy_conv_w[i], row(hy_conv_b[i]))
        w3f, w3b = _hy_tap_weights(hy_f_w3[i])
        filt = _hy_filter(feat, _pad2(hy_f_w1[i], LANE, LANE), _pad_cols(row(hy_f_b1[i]), LANE),
                          _pad_cols(row(hy_f_freq[i]), LANE), _pad2(hy_f_w2[i], LANE, LANE),
                          _pad_cols(row(hy_f_b2[i]), LANE), w3f, w3b, decay)

        wq, wkv = _mla_weights(mla_w_uq[i], mla_w_ukv[i])
        y_a = _mla(mla_in.reshape(bsz, seq, MLA_IN_W), misc.reshape(bsz, seq, MISC_W), ct, st,
                   row(mla_q_norm[i]), row(mla_kv_norm[i]), wq, wkv)

        y_c = _swa(swa_in.reshape(bsz, seq, SWA_IN_W), swa_sink[i])

        y_bt = _hy_conv(uct.reshape(HY_IN_W, bsz, seq), filt, hy_bias[i])

        nch = seq // SSD_CHUNK
        dt_rows = jnp.transpose(dt_t.reshape(2 * SSD_HEADS, bsz, nch, SSD_CHUNK), (1, 2, 0, 3))
        dt_rows = dt_rows.reshape(bsz, nch * 2 * SSD_HEADS, SSD_CHUNK)
        per_row = lambda v: jnp.tile(v.reshape(-1, 1), (nch, 1))
        dskip = jnp.repeat(ssd_d[i], SSD_HEAD_DIM, axis=1)
        y_d = _ssd(ssd_in.reshape(bsz, seq, SSD_IN_W), dt_rows, ssd_conv_w[i], row(ssd_conv_b[i]),
                   per_row(ssd_dt_bias[i]), per_row(ssd_a_log[i]), dskip)

        flat = lambda y: y.reshape(n, GROUP_WIDTH)
        tail_params = (row(mix_norm_g[i]), w_out[i].astype(BF16), row(ln1_g[i]), row(ln1_b[i]),
                       ffn_w_gate[i].astype(BF16), ffn_w_up[i].astype(BF16), ffn_conv_w[i], row(ffn_conv_b[i]),
                       ffn_w_down[i].astype(BF16), row(ln2_g[i]), row(ln2_b[i]),
                       ple_w_proj[i].astype(BF16), ple_w_gate[i].astype(BF16), row(ple_b_gate[i]),
                       row(ln3_g[i]), row(ln3_b[i]))
        h = _tail((flat(y_a), y_bt.reshape(HY_WIDTH, n), flat(y_c), flat(y_d)), h,
                  p.reshape(DEPTH, n, PLE_DIM), i, seq, tail_params)
    return h.reshape(bsz, seq, d)
```

```python
import functools
import math

import jax
import jax.numpy as jnp
import numpy as np
from jax import lax
from jax.experimental import pallas as pl
from jax.experimental.pallas import tpu as pltpu

F32 = jnp.float32
BF16 = jnp.bfloat16

D_MODEL = 1024
DEPTH = 2
PLE_DIM = 256
GROUP_WIDTH = 256
N_MIX = 4
HYENA_SLOT = 1
BLOCK = 128

MLA_HEADS = 4
MLA_Q_RANK = 256
MLA_KV_RANK = 128
MLA_NOPE = 64
MLA_ROPE = 32
MLA_V = 64
ROPE_THETA = 10000.0

HY_WIDTH = 256
HY_ORDER = 2
HY_EMB = 33
HY_BANDS = 16
HY_FAST_DECAY = 0.3
HY_SLOW_DECAY = 1.5
HY_TARGET = 1e-2
HY_BLK = 256

SWA_HEADS = 4
SWA_KV_HEADS = 2
SWA_HEAD_DIM = 64
SWA_WINDOW = 128
SWA_QB = 256

SSD_D_INNER = 256
SSD_HEAD_DIM = 64
SSD_HEADS = 4
SSD_GROUPS = 2
SSD_STATE = 128
SSD_CHUNK = 128

D_FF = 2816
LN_EPS = 1e-5
RMS_EPS = 1e-6
NEG_INF = -1e30
ALPHA = (2.0 * DEPTH) ** 0.25

LANE = 128
SUBLANE = 8
VMEM_LIMIT = 56 * 1024 * 1024

MLA_IN_W = 384
MISC_W = 128
HY_IN_W = 768
SWA_IN_W = 512
SSD_IN_W = 1024
IN_WIDTHS = (MLA_IN_W, MISC_W, HY_IN_W, SWA_IN_W, SSD_IN_W)
KR_LANE = MLA_NOPE

ROW_TILE = 512
FFN_HALO = 16
FFN_CHUNK = 1408


def _cparams(sem):
    return pltpu.CompilerParams(dimension_semantics=sem, vmem_limit_bytes=VMEM_LIMIT)


def _resident(shape):
    nd = len(shape)
    return pl.BlockSpec(shape, lambda *_: (0,) * nd, pipeline_mode=pl.Buffered(1))


def _layer_norm(x, g, b):
    mu = jnp.mean(x, -1, keepdims=True)
    xc = x - mu
    var = jnp.mean(xc * xc, -1, keepdims=True)
    return xc * lax.rsqrt(var + LN_EPS) * g + b


def _rms(x, g):
    return x * lax.rsqrt(jnp.mean(x * x, -1, keepdims=True) + RMS_EPS) * g


def _silu(x):
    return x / (1.0 + jnp.exp(-x))


def _dot(a, b):
    return jnp.dot(a, b, preferred_element_type=F32)


def _dot_nt(a, b):
    return lax.dot_general(a, b, (((1,), (1,)), ((), ())), preferred_element_type=F32)


def _dot_hi(a, b):
    return jnp.dot(a, b, preferred_element_type=F32, precision=lax.Precision.HIGHEST)


def _dot_nt_hi(a, b):
    return lax.dot_general(a, b, (((1,), (1,)), ((), ())), preferred_element_type=F32,
                           precision=lax.Precision.HIGHEST)


def _in_proj_kernel(apply_ln, *refs):
    if apply_ln:
        x_ref, g_ref, b_ref, w_ref, h_ref = refs[:5]
        out_refs = refs[5:]
        h = _layer_norm(x_ref[...], g_ref[...], b_ref[...])
        h_ref[...] = h
    else:
        x_ref, w_ref = refs[:2]
        out_refs = refs[2:]
        h = x_ref[...]
    u = _dot(h.astype(BF16), w_ref[...])
    o = 0
    for ref, w in zip(out_refs, IN_WIDTHS):
        ref[...] = u[:, o:o + w]
        o += w
    out_refs[-1][...] = u[:, MLA_IN_W:MLA_IN_W + MISC_W].T[0:2 * SSD_HEADS, :]


def _in_proj(x, w, ln=None):
    n = x.shape[0]
    row = lambda wd: pl.BlockSpec((ROW_TILE, wd), lambda i: (i, 0))
    outs = [jax.ShapeDtypeStruct((n, wd), F32) for wd in IN_WIDTHS]
    outs.append(jax.ShapeDtypeStruct((2 * SSD_HEADS, n), F32))
    out_specs = [row(wd) for wd in IN_WIDTHS] + [pl.BlockSpec((2 * SSD_HEADS, ROW_TILE), lambda i: (0, i))]
    if ln is not None:
        args = (x, ln[0], ln[1], w)
        in_specs = [row(D_MODEL), _resident((1, D_MODEL)), _resident((1, D_MODEL)), _resident(w.shape)]
        outs = [jax.ShapeDtypeStruct((n, D_MODEL), F32)] + outs
        out_specs = [row(D_MODEL)] + out_specs
    else:
        args = (x, w)
        in_specs = [row(D_MODEL), _resident(w.shape)]
    return pl.pallas_call(
        functools.partial(_in_proj_kernel, ln is not None),
        out_shape=outs, grid=(n // ROW_TILE,), in_specs=in_specs, out_specs=out_specs,
        compiler_params=_cparams(("parallel",)), name="in_proj")(*args)


MLA_TQ = 512


def _mla_kernel(x_ref, misc_ref, ct_ref, st_ref, gq_ref, gkv_ref, wq_ref, wkv_ref, o_ref,
                q_scr, k_scr, v_scr, e_scr, l_scr):
    seq = x_ref.shape[0]
    scale = (MLA_NOPE + MLA_ROPE) ** -0.5 * math.log2(math.e)
    half = MLA_ROPE // 2
    ct = ct_ref[...]
    st = st_ref[...]
    cqn = _rms(x_ref[:, 0:256], gq_ref[...]).astype(BF16)
    ckvn = _rms(x_ref[:, 256:384], gkv_ref[...]).astype(BF16)
    lane = lax.broadcasted_iota(jnp.int32, (1, LANE), 1)
    on_rope = (lane >= KR_LANE) & (lane < KR_LANE + MLA_ROPE)

    def rotate(x):
        return x * ct + pltpu.roll(x, LANE - half, 1) * st

    k_rope = jnp.where(on_rope, rotate(misc_ref[...]), 0.0)
    for h in range(MLA_HEADS):
        sl = slice(h * LANE, (h + 1) * LANE)
        q_scr[:, sl] = (rotate(_dot(cqn, wq_ref[:, sl])) * scale).astype(BF16)
        k_scr[:, sl] = (_dot(ckvn, wkv_ref[:, sl]) + k_rope).astype(BF16)
        vsl = slice(MLA_HEADS * LANE + h * LANE, MLA_HEADS * LANE + (h + 1) * LANE)
        v_scr[:, sl] = _dot(ckvn, wkv_ref[:, vsl]).astype(BF16)

    def tile_rows(t):
        start = t * MLA_TQ if isinstance(t, int) else pl.multiple_of(t * MLA_TQ, MLA_TQ)
        return pl.ds(start, MLA_TQ)

    def scores(t, h):
        sl = slice(h * LANE, (h + 1) * LANE)
        s = _dot_nt(q_scr[tile_rows(t), sl], k_scr[:, sl])
        e = jnp.exp2(s - jnp.max(s, -1, keepdims=True))
        l_scr[h % 2] = jnp.sum(e, -1, keepdims=True)
        e_scr[h % 2] = e.astype(BF16)

    def values(t, h):
        sl = slice(h * LANE, (h + 1) * LANE)
        o = _dot(e_scr[h % 2], v_scr[:, sl]) / l_scr[h % 2]
        out = (tile_rows(t), slice((h // 2) * LANE, (h // 2 + 1) * LANE))
        if h % 2 == 0:
            o_ref[out] = o
        else:
            o_ref[out] += o

    def q_tile(t, last):
        for h in range(MLA_HEADS):
            values(t, h)
            if h + 1 < MLA_HEADS:
                scores(t, h + 1)
            elif not last:
                scores(t + 1, 0)

    def body(t, carry):
        q_tile(t, False)
        return carry

    n_tiles = seq // MLA_TQ
    scores(0, 0)
    lax.fori_loop(0, n_tiles - 1, body, 0)
    q_tile(n_tiles - 1, True)


def _mla(mla_in, misc, ct, st, gq, gkv, wq, wkv):
    bsz, seq, _ = mla_in.shape
    return pl.pallas_call(
        _mla_kernel,
        out_shape=jax.ShapeDtypeStruct((bsz, seq, GROUP_WIDTH), F32),
        grid=(bsz,),
        in_specs=[pl.BlockSpec((None, seq, MLA_IN_W), lambda b: (b, 0, 0)),
                  pl.BlockSpec((None, seq, MISC_W), lambda b: (b, 0, 0)),
                  _resident(ct.shape), _resident(st.shape), _resident(gq.shape), _resident(gkv.shape),
                  _resident(wq.shape), _resident(wkv.shape)],
        out_specs=pl.BlockSpec((None, seq, GROUP_WIDTH), lambda b: (b, 0, 0)),
        scratch_shapes=[pltpu.VMEM((seq, MLA_HEADS * LANE), BF16)] * 3
                       + [pltpu.VMEM((2, MLA_TQ, seq), BF16), pltpu.VMEM((2, MLA_TQ, 1), F32)],
        compiler_params=_cparams(("parallel",)), name="mla")(mla_in, misc, ct, st, gq, gkv, wq, wkv)


def _hy_filter_kernel(feat_ref, w1_ref, b1_ref, fr_ref, w2_ref, b2_ref, w3f_ref, w3b_ref, decay_ref, o_ref):
    seq = feat_ref.shape[0] // 2
    fr = fr_ref[...]
    h = jnp.sin(fr * (_dot_hi(feat_ref[...], w1_ref[...]) + b1_ref[...]))
    h = jnp.sin(fr * (_dot_hi(h, w2_ref[...]) + b2_ref[...]))
    for o in range(HY_ORDER):
        rows = slice(o * HY_WIDTH, (o + 1) * HY_WIDTH)
        o_ref[o, :, 0:seq] = _dot_nt_hi(w3b_ref[rows, :], h[0:seq]) * decay_ref[:, 0:seq]
        o_ref[o, :, seq:2 * seq] = _dot_nt_hi(w3f_ref[rows, :], h[seq:2 * seq]) * decay_ref[:, seq:2 * seq]


def _hy_filter(feat, w1, b1, fr, w2, b2, w3f, w3b, decay):
    return pl.pallas_call(
        _hy_filter_kernel,
        out_shape=jax.ShapeDtypeStruct((HY_ORDER, HY_WIDTH, feat.shape[0]), F32),
        compiler_params=pltpu.CompilerParams(vmem_limit_bytes=VMEM_LIMIT),
        name="hy_filter")(feat, w1, b1, fr, w2, b2, w3f, w3b, decay)


def _shift_rows(x, row, seq):
    prev = jnp.where(row == 0, 0.0, pltpu.roll(x, 1, 0))
    nxt = jnp.where(row == seq - 1, 0.0, pltpu.roll(x, seq - 1, 0))
    return prev, nxt


def _dwconv3(x, w, b, row, seq):
    prev, nxt = _shift_rows(x, row, seq)
    return prev * w[0:1] + x * w[1:2] + nxt * w[2:3] + b


def _hy_pre_kernel(u_ref, w_ref, b_ref, o_ref):
    seq = u_ref.shape[0]
    row = lax.broadcasted_iota(jnp.int32, (seq, HY_WIDTH), 0)
    for k in range(HY_ORDER + 1):
        sl = slice(k * HY_WIDTH, (k + 1) * HY_WIDTH)
        o_ref[sl, :] = _dwconv3(u_ref[:, sl], w_ref[:, sl], b_ref[:, sl], row, seq).T


def _hy_pre(hy_u, w, b):
    bsz, seq, wd = hy_u.shape
    return pl.pallas_call(
        _hy_pre_kernel,
        out_shape=jax.ShapeDtypeStruct((wd, bsz * seq), F32),
        grid=(bsz,),
        in_specs=[pl.BlockSpec((None, seq, wd), lambda b: (b, 0, 0)), _resident(w.shape), _resident(b.shape)],
        out_specs=pl.BlockSpec((wd, seq), lambda b: (0, b)),
        compiler_params=_cparams(("parallel",)), name="hy_pre")(hy_u, w, b)


HY_ROWS16 = 16
HY_CH = 4


def _hy_conv_kernel(bias_ref, v_ref, x1_ref, x2_ref, f_ref, _after_ref, o_ref, r_scr):
    _, bsz, seq = v_ref.shape
    nblk = seq // HY_BLK
    c0 = pl.program_id(0) * HY_CH

    def build_toeplitz(slot, taps):
        k16 = jnp.broadcast_to(taps, (HY_ROWS16, 2 * seq))
        k16 = pltpu.roll(k16, 0, 1, stride=1, stride_axis=0)
        for g in range(LANE // HY_ROWS16):
            rows = slice(g * HY_ROWS16, (g + 1) * HY_ROWS16)
            r_scr[slot, rows, :] = (k16 if g == 0 else pltpu.roll(k16, g * HY_ROWS16, 1)).astype(BF16)
        r_scr[slot, LANE:2 * LANE, LANE:] = r_scr[slot, 0:LANE, 0:2 * seq - LANE]

    def long_conv(z, slot):
        zr = jnp.concatenate([z[:, j * HY_BLK:(j + 1) * HY_BLK] for j in range(nblk)], axis=0).astype(BF16)
        acc = [None] * nblk
        for d in range(-(nblk - 1), nblk):
            lo_in, hi_in = max(0, -d), nblk - max(0, d)
            w_d = r_scr[slot, :, seq + d * HY_BLK: seq + (d + 1) * HY_BLK]
            part = _dot(zr[lo_in * bsz:hi_in * bsz, :], w_d)
            for j in range(lo_in, hi_in):
                piece = part[(j - lo_in) * bsz:(j - lo_in + 1) * bsz, :]
                acc[j + d] = piece if acc[j + d] is None else acc[j + d] + piece
        return jnp.concatenate(acc, axis=1)

    f_row = (pl.program_id(0) % (SUBLANE // HY_CH)) * HY_CH
    for ch in range(HY_CH):
        for order in range(HY_ORDER):
            build_toeplitz(ch * HY_ORDER + order, f_ref[order, pl.ds(f_row + ch, 1), :])
    for ch in range(HY_CH):
        z0 = v_ref[ch]
        z1 = x1_ref[ch] * (long_conv(z0, ch * HY_ORDER) + bias_ref[0, c0 + ch] * z0)
        o_ref[ch] = x2_ref[ch] * (long_conv(z1, ch * HY_ORDER + 1) + bias_ref[1, c0 + ch] * z1)


def _hy_conv(uct, filt, bias, after):
    _, bsz, seq = uct.shape
    per_slab = HY_WIDTH // HY_CH
    chan = lambda k: pl.BlockSpec((HY_CH, bsz, seq), lambda c, k=k: (c + k * per_slab, 0, 0))
    return pl.pallas_call(
        _hy_conv_kernel,
        out_shape=jax.ShapeDtypeStruct((HY_WIDTH, bsz, seq), F32),
        grid=(per_slab,),
        in_specs=[pl.BlockSpec(memory_space=pltpu.SMEM), chan(0), chan(1), chan(2),
                  pl.BlockSpec((HY_ORDER, SUBLANE, 2 * seq), lambda c: (0, c * HY_CH // SUBLANE, 0)),
                  pl.BlockSpec(memory_space=pl.ANY)],
        out_specs=pl.BlockSpec((HY_CH, bsz, seq), lambda c: (c, 0, 0)),
        scratch_shapes=[pltpu.VMEM((HY_CH * HY_ORDER, HY_BLK, 2 * seq), BF16)],
        compiler_params=_cparams(("parallel",)), name="hy_conv")(bias, uct, uct, uct, filt, after)


def _swa_kernel(sink_ref, x_ref, o_ref, k_scr, vlo_scr, vhi_scr, bias_scr):
    seq = x_ref.shape[0]
    qb = SWA_QB
    nb = seq // qb
    win = qb + 2 * BLOCK
    grp = SWA_HEADS // SWA_KV_HEADS
    log2e = math.log2(math.e)
    scale = SWA_HEAD_DIM ** -0.5 * log2e
    kvw = SWA_KV_HEADS * LANE
    zeros = jnp.zeros((BLOCK, kvw), BF16)
    lane = lax.broadcasted_iota(jnp.int32, (1, LANE), 1)
    lo_half = lane < SWA_HEAD_DIM
    for scr in (k_scr, vlo_scr, vhi_scr):
        scr[0:BLOCK, :] = zeros
        scr[BLOCK + seq:2 * BLOCK + seq, :] = zeros
    body = slice(BLOCK, BLOCK + seq)
    k_both = x_ref[:, 256:256 + LANE]
    k_flip = pltpu.roll(k_both, SWA_HEAD_DIM, 1)
    k_scr[body, 0:LANE] = jnp.where(lo_half, k_both, k_flip).astype(BF16)
    k_scr[body, LANE:2 * LANE] = jnp.where(lo_half, k_flip, k_both).astype(BF16)
    v_both = x_ref[:, 256 + LANE:256 + 2 * LANE]
    v_flip = pltpu.roll(v_both, SWA_HEAD_DIM, 1)
    vlo_scr[body, 0:LANE] = jnp.where(lo_half, v_both, 0.0).astype(BF16)
    vhi_scr[body, 0:LANE] = jnp.where(lo_half, 0.0, v_flip).astype(BF16)
    vlo_scr[body, LANE:2 * LANE] = jnp.where(lo_half, v_flip, 0.0).astype(BF16)
    vhi_scr[body, LANE:2 * LANE] = jnp.where(lo_half, 0.0, v_both).astype(BF16)

    qi = lax.broadcasted_iota(jnp.int32, (grp * qb, win), 0)
    ji = lax.broadcasted_iota(jnp.int32, (grp * qb, win), 1)
    upper = qi >= qb
    dist = jnp.abs(jnp.where(upper, qi - qb, qi) + BLOCK - ji)
    slopes = [(2.0 ** (-8.0 / SWA_HEADS)) ** (i + 1) * log2e for i in range(SWA_HEADS)]
    for j in range(SWA_KV_HEADS):
        slope = jnp.where(upper, slopes[grp * j + 1], slopes[grp * j])
        bias_scr[j] = jnp.where(dist <= SWA_WINDOW, -slope * dist.astype(F32), NEG_INF)
    col = lax.broadcasted_iota(jnp.int32, (1, win), 1)
    row_up = lax.broadcasted_iota(jnp.int32, (grp * qb, 1), 0) >= qb

    def block(n, edge):
        r0 = n * qb if isinstance(n, int) else pl.multiple_of(n * qb, qb)
        for j in range(SWA_KV_HEADS):
            sl = slice(j * LANE, (j + 1) * LANE)
            qp = x_ref[pl.ds(r0, qb), sl] * scale
            q2 = jnp.concatenate([jnp.where(lo_half, qp, 0.0), jnp.where(lo_half, 0.0, qp)], 0).astype(BF16)
            sc = _dot_nt(q2, k_scr[pl.ds(r0, win), sl]) + bias_scr[j]
            if edge == "first":
                sc = jnp.where(col >= BLOCK, sc, NEG_INF)
            elif edge == "last":
                sc = jnp.where(col < qb + BLOCK, sc, NEG_INF)
            sk = jnp.where(row_up, sink_ref[grp * j + 1], sink_ref[grp * j]) * log2e
            m = jnp.maximum(jnp.max(sc, -1, keepdims=True), sk)
            e = jnp.exp2(sc - m)
            den = jnp.sum(e, -1, keepdims=True) + jnp.exp2(sk - m)
            eb = e.astype(BF16)
            o_lo = _dot(eb[0:qb], vlo_scr[pl.ds(r0, win), sl]) / den[0:qb]
            o_hi = _dot(eb[qb:], vhi_scr[pl.ds(r0, win), sl]) / den[qb:]
            o_ref[pl.ds(r0, qb), sl] = o_lo + o_hi

    def inner(n, carry):
        block(n, None)
        return carry

    block(0, "first")
    lax.fori_loop(1, nb - 1, inner, 0, unroll=2)
    block(nb - 1, "last")


def _swa(swa_in, sink):
    bsz, seq, _ = swa_in.shape
    kv_scr = pltpu.VMEM((seq + 2 * BLOCK, SWA_KV_HEADS * LANE), BF16)
    grp = SWA_HEADS // SWA_KV_HEADS
    return pl.pallas_call(
        _swa_kernel,
        out_shape=jax.ShapeDtypeStruct((bsz, seq, GROUP_WIDTH), F32),
        grid=(bsz,),
        in_specs=[pl.BlockSpec(memory_space=pltpu.SMEM),
                  pl.BlockSpec((None, seq, SWA_IN_W), lambda b: (b, 0, 0))],
        out_specs=pl.BlockSpec((None, seq, GROUP_WIDTH), lambda b: (b, 0, 0)),
        scratch_shapes=[kv_scr, kv_scr, kv_scr,
                        pltpu.VMEM((SWA_KV_HEADS, grp * SWA_QB, SWA_QB + 2 * BLOCK), F32)],
        compiler_params=_cparams(("parallel",)), name="swa")(sink, swa_in)


def _softplus(x):
    return jnp.maximum(x, 0.0) + jnp.log1p(jnp.exp(-jnp.abs(x)))


def _split3(x):
    hi = x.astype(BF16)
    rest = x - hi.astype(F32)
    mid = rest.astype(BF16)
    return hi, mid, (rest - mid.astype(F32)).astype(BF16)


def _ssd_kernel(x_ref, dtr_ref, cw_ref, cb_ref, dtb_col_ref, alog_col_ref, dskip_ref, o_ref,
                act_scr, yf_scr, yb_scr, rows_scr, cols_scr):
    seq = x_ref.shape[0]
    q = SSD_CHUNK
    nch = seq // q
    nh = 2 * SSD_HEADS
    hpg = SSD_HEADS // SSD_GROUPS
    row = lax.broadcasted_iota(jnp.int32, (seq, GROUP_WIDTH), 0)
    for k in range(3):
        sl = slice(k * GROUP_WIDTH, (k + 1) * GROUP_WIDTH)
        xin = x_ref[:, 256 + k * GROUP_WIDTH:256 + (k + 1) * GROUP_WIDTH]
        act_scr[:, sl] = _silu(_dwconv3(xin, cw_ref[:, sl], cb_ref[:, sl], row, seq))

    ti = lax.broadcasted_iota(jnp.int32, (q, q), 0)
    si = lax.broadcasted_iota(jnp.int32, (q, q), 1)
    lower = si <= ti
    upper = si >= ti
    tri_l = jnp.where(lower, 1.0, 0.0).astype(BF16)
    tri_u = jnp.where(upper, 1.0, 0.0).astype(BF16)
    lane = lax.broadcasted_iota(jnp.int32, (1, LANE), 1)
    first = lane < SSD_HEAD_DIM

    dt_all = _softplus(dtr_ref[...] + dtb_col_ref[...])
    da_parts = _split3(dt_all * -jnp.exp(alog_col_ref[...]))
    row_tri = jnp.concatenate([tri_u, tri_l], axis=1)
    col_tri = jnp.concatenate([tri_l, tri_u], axis=0)
    rows_scr[...] = sum(_dot(part, row_tri) for part in da_parts)
    cols_scr[0:2 * q, :] = sum(_dot_nt(col_tri, part) for part in da_parts)
    cols_scr[2 * q:3 * q, :] = dt_all.T

    def per_head(col, h0):
        return jnp.where(first, col[:, h0:h0 + 1], col[:, h0 + 1:h0 + 2])

    def one_chunk(direction, c, carry, y_scr):
        hb = direction * SSD_HEADS
        keep = lower if direction == 0 else upper
        edge = q - 1 if direction == 0 else 0
        rows = slice(c * q, (c + 1) * q)
        heads = slice(c * nh, (c + 1) * nh)
        cs_col = cols_scr[direction * q:(direction + 1) * q, heads]
        cs_row = rows_scr[heads, direction * q:(direction + 1) * q]
        dt_col = cols_scr[2 * q:3 * q, heads]
        new = []
        for g in range(SSD_GROUPS):
            gs = slice(g * LANE, (g + 1) * LANE)
            h0 = hb + g * hpg
            xg = act_scr[rows, gs]
            bg = act_scr[rows, GROUP_WIDTH + g * LANE:GROUP_WIDTH + (g + 1) * LANE]
            cg = act_scr[rows, 2 * GROUP_WIDTH + g * LANE:2 * GROUP_WIDTH + (g + 1) * LANE]
            xdt = xg * per_head(dt_col, h0)
            cb = _dot_nt(cg.astype(BF16), bg.astype(BF16))
            y = None
            for r in range(hpg):
                seg = cs_col[:, h0 + r:h0 + r + 1] - cs_row[h0 + r:h0 + r + 1, :]
                lmat = jnp.exp(jnp.where(keep, seg, NEG_INF))
                xh = jnp.where(first if r == 0 else ~first, xdt, 0.0)
                part = _dot((cb * lmat).astype(BF16), xh.astype(BF16))
                y = part if y is None else y + part
            cs_g = per_head(cs_col, h0)
            total = cs_g[edge:edge + 1, :]
            y_scr[rows, gs] = y + _dot(cg.astype(BF16), carry[g].astype(BF16)) * jnp.exp(cs_g)
            st = _dot(bg.T.astype(BF16), (xdt * jnp.exp(total - cs_g)).astype(BF16))
            new.append(carry[g] * jnp.exp(total) + st)
        return new

    fwd = [jnp.zeros((SSD_STATE, LANE), F32) for _ in range(SSD_GROUPS)]
    bwd = [jnp.zeros((SSD_STATE, LANE), F32) for _ in range(SSD_GROUPS)]
    for i in range(nch):
        fwd = one_chunk(0, i, fwd, yf_scr)
        bwd = one_chunk(1, nch - 1 - i, bwd, yb_scr)

    dsum = dskip_ref[0:1, :] + dskip_ref[1:2, :]
    y = yf_scr[...] + yb_scr[...] + dsum * act_scr[:, 0:GROUP_WIDTH]
    o_ref[...] = y * _silu(x_ref[:, 0:GROUP_WIDTH])


def _ssd(ssd_in, dt_rows, cw, cb, dtb_col, alog_col, dskip):
    bsz, seq, _ = ssd_in.shape
    q = SSD_CHUNK
    nch = seq // q
    nh = 2 * SSD_HEADS
    small = (cw, cb, dtb_col, alog_col, dskip)
    return pl.pallas_call(
        _ssd_kernel,
        out_shape=jax.ShapeDtypeStruct((bsz, seq, GROUP_WIDTH), F32),
        grid=(bsz,),
        in_specs=[pl.BlockSpec((None, seq, SSD_IN_W), lambda b: (b, 0, 0)),
                  pl.BlockSpec((None, nch * nh, q), lambda b: (b, 0, 0))]
                 + [_resident(a.shape) for a in small],
        out_specs=pl.BlockSpec((None, seq, GROUP_WIDTH), lambda b: (b, 0, 0)),
        scratch_shapes=[pltpu.VMEM((seq, 3 * GROUP_WIDTH), F32)] + [pltpu.VMEM((seq, GROUP_WIDTH), F32)] * 2
                       + [pltpu.VMEM((nch * nh, 2 * q), F32), pltpu.VMEM((3 * q, nch * nh), F32)],
        compiler_params=_cparams(("parallel",)), name="ssd")(ssd_in, dt_rows, *small)


def _tail_kernel(tiles_per_seq, *refs):
    y_refs, refs = refs[:12], refs[12:]
    (hp_ref, h_ref, hn_ref, p_ref, mg_ref, wo_ref, l1g_ref, l1b_ref, wg_ref, wu_ref, cw_ref, cb_ref, wd_ref,
     lg_ref, lb_ref, pwp_ref, pwg_ref, pbg_ref, plg_ref, plb_ref, o_ref) = refs
    i = pl.program_id(0)
    ext = ROW_TILE + 2 * FFN_HALO
    main = slice(FFN_HALO, FFN_HALO + ROW_TILE)
    parts = []
    for k in range(N_MIX):
        prev_ref, main_ref, next_ref = y_refs[3 * k:3 * k + 3]
        if k == HYENA_SLOT:
            rows = [prev_ref[...].T[LANE - FFN_HALO:, :], main_ref[...].T, next_ref[...].T[:FFN_HALO, :]]
        else:
            rows = [prev_ref[...], main_ref[...], next_ref[...]]
        y = jnp.concatenate(rows, axis=0)
        parts.append(_rms(y, mg_ref[:, k * GROUP_WIDTH:(k + 1) * GROUP_WIDTH]).astype(BF16))
    h_in = jnp.concatenate([hp_ref[...], h_ref[...], hn_ref[...]], axis=0)
    h_ext = _layer_norm(ALPHA * h_in + _dot(jnp.concatenate(parts, axis=1), wo_ref[...]),
                        l1g_ref[...], l1b_ref[...])
    hext = h_ext.astype(BF16)
    h = h_ext[main]
    hb = hext[main]
    row = lax.broadcasted_iota(jnp.int32, (ROW_TILE, 1), 0)
    keep_prev = jnp.where((row == 0) & (i % tiles_per_seq == 0), 0.0, 1.0)
    keep_next = jnp.where((row == ROW_TILE - 1) & (i % tiles_per_seq == tiles_per_seq - 1), 0.0, 1.0)
    f = None
    for c0 in range(0, D_FF, FFN_CHUNK):
        sl = slice(c0, c0 + FFN_CHUNK)
        gate = _dot(hext, wg_ref[:, sl])
        g_prev = pltpu.roll(gate, 1, 0)[main]
        g_next = pltpu.roll(gate, ext - 1, 0)[main]
        g_cur = gate[main]
        conv = (g_prev * keep_prev * cw_ref[0:1, sl] + g_cur * cw_ref[1:2, sl]
                + g_next * keep_next * cw_ref[2:3, sl] + cb_ref[:, sl])
        act = (_silu(conv) * _dot(hb, wu_ref[:, sl])).astype(BF16)
        part = _dot(act, wd_ref[sl, :])
        f = part if f is None else f + part
    h = _layer_norm(ALPHA * h + f, lg_ref[...], lb_ref[...])
    gate = _dot(h.astype(BF16), pwg_ref[...]) + pbg_ref[...]
    e = _dot(p_ref[...].astype(BF16), pwp_ref[...]) / (1.0 + jnp.exp(-gate))
    o_ref[...] = _layer_norm(ALPHA * h + e, plg_ref[...], plb_ref[...])


def _halo_specs(width, n):
    per_tile = ROW_TILE // FFN_HALO
    last = n // FFN_HALO - 1
    return [pl.BlockSpec((FFN_HALO, width), lambda i: (jnp.maximum(i * per_tile - 1, 0), 0)),
            pl.BlockSpec((ROW_TILE, width), lambda i: (i, 0)),
            pl.BlockSpec((FFN_HALO, width), lambda i: (jnp.minimum((i + 1) * per_tile, last), 0))]


def _halo_specs_channel_major(width, n):
    per_tile = ROW_TILE // LANE
    last = n // LANE - 1
    return [pl.BlockSpec((width, LANE), lambda i: (0, jnp.maximum(i * per_tile - 1, 0))),
            pl.BlockSpec((width, ROW_TILE), lambda i: (0, i)),
            pl.BlockSpec((width, LANE), lambda i: (0, jnp.minimum((i + 1) * per_tile, last)))]


def _tail(ys, h, p, layer, seq, small):
    n = h.shape[0]
    y_specs, y_args = [], []
    for k, y in enumerate(ys):
        y_specs += _halo_specs_channel_major(GROUP_WIDTH, n) if k == HYENA_SLOT else _halo_specs(GROUP_WIDTH, n)
        y_args += [y, y, y]
    return pl.pallas_call(
        functools.partial(_tail_kernel, seq // ROW_TILE),
        out_shape=jax.ShapeDtypeStruct((n, D_MODEL), F32),
        grid=(n // ROW_TILE,),
        in_specs=y_specs + _halo_specs(D_MODEL, n)
                 + [pl.BlockSpec((None, ROW_TILE, PLE_DIM), lambda i: (layer, i, 0))]
                 + [_resident(a.shape) for a in small],
        out_specs=pl.BlockSpec((ROW_TILE, D_MODEL), lambda i: (i, 0)),
        compiler_params=_cparams(("parallel",)), name="tail")(*y_args, h, h, h, p, *small)


def _pad_cols(w, width):
    return jnp.pad(w, ((0, 0), (0, width - w.shape[1])))


def _in_proj_weight(w):
    o = np.cumsum([0, 256, 128, 32, 768, 256, 128, 128, 256, 768, 8])
    cq, ckv, kr, hy, sq, sk, sv, z, xbc, dt = (w[:, o[i]:o[i + 1]] for i in range(10))
    zc = lambda n: jnp.zeros((w.shape[0], n), w.dtype)
    half = MLA_ROPE // 2
    misc = [dt, zc(KR_LANE - dt.shape[1]), kr, kr[:, :half], zc(MISC_W - KR_LANE - MLA_ROPE - half)]
    return jnp.concatenate([cq, ckv] + misc + [hy, sq, sk, sv, z, xbc], 1).astype(BF16)


def _mla_weights(w_uq, w_ukv):
    r = w_uq.shape[0]
    half = MLA_ROPE // 2
    qh = w_uq.reshape(r, MLA_HEADS, MLA_NOPE + MLA_ROPE)
    zq = jnp.zeros((r, MLA_HEADS, LANE - MLA_NOPE - MLA_ROPE - half), w_uq.dtype)
    wq = jnp.concatenate([qh, qh[..., MLA_NOPE:MLA_NOPE + half], zq], -1).reshape(r, MLA_HEADS * LANE)
    rk = w_ukv.shape[0]
    kvh = w_ukv.reshape(rk, MLA_HEADS, MLA_NOPE + MLA_V)
    zk = jnp.zeros((rk, MLA_HEADS, LANE - MLA_NOPE), w_ukv.dtype)
    wk = jnp.concatenate([kvh[..., :MLA_NOPE], zk], -1).reshape(rk, MLA_HEADS * LANE)
    v = kvh[..., MLA_NOPE:]
    zv = jnp.zeros_like(v)
    odd = (jnp.arange(MLA_HEADS) % 2 == 1)[None, :, None]
    wv = jnp.concatenate([jnp.where(odd, zv, v), jnp.where(odd, v, zv)], -1).reshape(rk, MLA_HEADS * LANE)
    return wq.astype(BF16), jnp.concatenate([wk, wv], 1).astype(BF16)


def _rope_tables(seq):
    f32 = np.float32
    inv_freq = (f32(ROPE_THETA) ** (-np.arange(0, MLA_ROPE, 2, dtype=f32) / f32(MLA_ROPE))).astype(f32)
    ang = np.arange(seq, dtype=f32)[:, None] * inv_freq[None, :]
    cos, sin = np.cos(ang), np.sin(ang)
    one = np.ones((seq, MLA_NOPE), f32)
    zero = lambda n: np.zeros((seq, n), f32)
    ct = np.concatenate([one, cos, cos, zero(LANE - MLA_NOPE - MLA_ROPE)], 1)
    st = np.concatenate([zero(MLA_NOPE), -sin, sin, zero(LANE - MLA_NOPE - MLA_ROPE)], 1)
    return jnp.asarray(ct), jnp.asarray(st)


def _hy_tables(seq):
    f32 = np.float32
    t = np.linspace(0.0, 1.0, seq, dtype=f32)[:, None]
    ang = (f32(2.0 * math.pi) * np.arange(seq, dtype=f32)[:, None] / f32(seq)).astype(f32)
    bands = np.linspace(1e-4, HY_BANDS - 1, HY_BANDS, dtype=f32)[None, :]
    feat = np.concatenate([t, np.cos(bands * ang), -np.sin(bands * ang)], -1).astype(f32)
    max_decay = math.log(HY_TARGET) / HY_FAST_DECAY
    min_decay = math.log(HY_TARGET) / HY_SLOW_DECAY
    deltas = np.linspace(min_decay, max_decay, HY_WIDTH, dtype=f32)
    decay = np.exp(-t * np.abs(deltas)[None, :]).astype(f32)
    m = np.arange(2 * seq)
    pos = np.where(m < seq, np.minimum(seq - m, seq - 1), m - seq)
    decay_lag = np.where((m == 0)[:, None], f32(0.0), decay[pos])
    feat_lag = np.pad(feat[pos], ((0, 0), (0, LANE - feat.shape[1])))
    return jnp.asarray(feat_lag), jnp.asarray(np.ascontiguousarray(decay_lag.T))


def _pad2(w, rows, cols):
    return jnp.pad(w, ((0, rows - w.shape[0]), (0, cols - w.shape[1])))


def _hy_tap_weights(w3):
    w = w3.reshape(w3.shape[0], HY_ORDER, 2, HY_WIDTH)
    rows = lambda d: _pad_cols(w[:, :, d].reshape(w3.shape[0], HY_ORDER * HY_WIDTH).T, LANE)
    return rows(0), rows(1)


def kernel(x, p, emb_ln_g, emb_ln_b, w_in, mla_q_norm, mla_kv_norm, mla_w_uq, mla_w_ukv, hy_conv_w, hy_conv_b, hy_f_w1, hy_f_b1, hy_f_freq, hy_f_w2, hy_f_b2, hy_f_w3, hy_bias, swa_sink, ssd_conv_w, ssd_conv_b, ssd_dt_bias, ssd_a_log, ssd_d, mix_norm_g, w_out, ln1_g, ln1_b, ffn_w_gate, ffn_w_up, ffn_conv_w, ffn_conv_b, ffn_w_down, ln2_g, ln2_b, ple_w_proj, ple_w_gate, ple_b_gate, ln3_g, ln3_b):
    bsz, seq, d = x.shape
    n = bsz * seq
    row = lambda v: v.reshape(1, -1)
    ct, st = _rope_tables(seq)
    feat, decay = _hy_tables(seq)
    h = x.reshape(n, d)
    for i in range(DEPTH):
        w_in_i = _in_proj_weight(w_in[i])
        if i == 0:
            h, mla_in, misc, hy_u, swa_in, ssd_in, dt_t = _in_proj(h, w_in_i, (row(emb_ln_g), row(emb_ln_b)))
        else:
            mla_in, misc, hy_u, swa_in, ssd_in, dt_t = _in_proj(h, w_in_i)

        uct = _hy_pre(hy_u.reshape(bsz, seq, HY_IN_W), hy_conv_w[i], row(hy_conv_b[i]))
        w3f, w3b = _hy_tap_weights(hy_f_w3[i])
        filt = _hy_filter(feat, _pad2(hy_f_w1[i], LANE, LANE), _pad_cols(row(hy_f_b1[i]), LANE),
                          _pad_cols(row(hy_f_freq[i]), LANE), _pad2(hy_f_w2[i], LANE, LANE),
                          _pad_cols(row(hy_f_b2[i]), LANE), w3f, w3b, decay)

        wq, wkv = _mla_weights(mla_w_uq[i], mla_w_ukv[i])
        y_a = _mla(mla_in.reshape(bsz, seq, MLA_IN_W), misc.reshape(bsz, seq, MISC_W), ct, st,
                   row(mla_q_norm[i]), row(mla_kv_norm[i]), wq, wkv)

        y_c = _swa(swa_in.reshape(bsz, seq, SWA_IN_W), swa_sink[i])

        y_bt = _hy_conv(uct.reshape(HY_IN_W, bsz, seq), filt, hy_bias[i], y_c)

        nch = seq // SSD_CHUNK
        dt_rows = jnp.transpose(dt_t.reshape(2 * SSD_HEADS, bsz, nch, SSD_CHUNK), (1, 2, 0, 3))
        dt_rows = dt_rows.reshape(bsz, nch * 2 * SSD_HEADS, SSD_CHUNK)
        per_row = lambda v: jnp.tile(v.reshape(-1, 1), (nch, 1))
        dskip = jnp.repeat(ssd_d[i], SSD_HEAD_DIM, axis=1)
        y_d = _ssd(ssd_in.reshape(bsz, seq, SSD_IN_W), dt_rows, ssd_conv_w[i], row(ssd_conv_b[i]),
                   per_row(ssd_dt_bias[i]), per_row(ssd_a_log[i]), dskip)

        flat = lambda y: y.reshape(n, GROUP_WIDTH)
        tail_params = (row(mix_norm_g[i]), w_out[i].astype(BF16), row(ln1_g[i]), row(ln1_b[i]),
                       ffn_w_gate[i].astype(BF16), ffn_w_up[i].astype(BF16), ffn_conv_w[i], row(ffn_conv_b[i]),
                       ffn_w_down[i].astype(BF16), row(ln2_g[i]), row(ln2_b[i]),
                       ple_w_proj[i].astype(BF16), ple_w_gate[i].astype(BF16), row(ple_b_gate[i]),
                       row(ln3_g[i]), row(ln3_b[i]))
        h = _tail((flat(y_a), y_bt.reshape(HY_WIDTH, n), flat(y_c), flat(y_d)), h,
                  p.reshape(DEPTH, n, PLE_DIM), i, seq, tail_params)
    return h.reshape(bsz, seq, d)
```

```python
import functools
import math

import jax
import jax.numpy as jnp
import numpy as np
from jax import lax
from jax.experimental import pallas as pl
from jax.experimental.pallas import tpu as pltpu

F32 = jnp.float32
BF16 = jnp.bfloat16

D_MODEL = 1024
DEPTH = 2
PLE_DIM = 256
GROUP_WIDTH = 256
N_MIX = 4
HYENA_SLOT = 1
BLOCK = 128

MLA_HEADS = 4
MLA_Q_RANK = 256
MLA_KV_RANK = 128
MLA_NOPE = 64
MLA_ROPE = 32
MLA_V = 64
ROPE_THETA = 10000.0

HY_WIDTH = 256
HY_ORDER = 2
HY_EMB = 33
HY_BANDS = 16
HY_FAST_DECAY = 0.3
HY_SLOW_DECAY = 1.5
HY_TARGET = 1e-2
HY_BLK = 256

SWA_HEADS = 4
SWA_KV_HEADS = 2
SWA_HEAD_DIM = 64
SWA_WINDOW = 128
SWA_QB = 256

SSD_D_INNER = 256
SSD_HEAD_DIM = 64
SSD_HEADS = 4
SSD_GROUPS = 2
SSD_STATE = 128
SSD_CHUNK = 128

D_FF = 2816
LN_EPS = 1e-5
RMS_EPS = 1e-6
NEG_INF = -1e30
ALPHA = (2.0 * DEPTH) ** 0.25

LANE = 128
SUBLANE = 8
VMEM_LIMIT = 56 * 1024 * 1024

MLA_IN_W = 384
MISC_W = 128
HY_IN_W = 768
SWA_IN_W = 512
SSD_IN_W = 1024
IN_WIDTHS = (MLA_IN_W, MISC_W, HY_IN_W, SWA_IN_W, SSD_IN_W)
KR_LANE = MLA_NOPE

ROW_TILE = 512
FFN_HALO = 16
FFN_CHUNK = 1408


def _cparams(sem):
    return pltpu.CompilerParams(dimension_semantics=sem, vmem_limit_bytes=VMEM_LIMIT)


def _resident(shape):
    nd = len(shape)
    return pl.BlockSpec(shape, lambda *_: (0,) * nd, pipeline_mode=pl.Buffered(1))


def _layer_norm(x, g, b):
    mu = jnp.mean(x, -1, keepdims=True)
    xc = x - mu
    var = jnp.mean(xc * xc, -1, keepdims=True)
    return xc * lax.rsqrt(var + LN_EPS) * g + b


def _rms(x, g):
    return x * lax.rsqrt(jnp.mean(x * x, -1, keepdims=True) + RMS_EPS) * g


def _silu(x):
    return x / (1.0 + jnp.exp(-x))


def _dot(a, b):
    return jnp.dot(a, b, preferred_element_type=F32)


def _dot_nt(a, b):
    return lax.dot_general(a, b, (((1,), (1,)), ((), ())), preferred_element_type=F32)


def _dot_hi(a, b):
    return jnp.dot(a, b, preferred_element_type=F32, precision=lax.Precision.HIGHEST)


def _dot_nt_hi(a, b):
    return lax.dot_general(a, b, (((1,), (1,)), ((), ())), preferred_element_type=F32,
                           precision=lax.Precision.HIGHEST)


def _in_proj_kernel(apply_ln, *refs):
    if apply_ln:
        x_ref, g_ref, b_ref, w_ref, h_ref = refs[:5]
        out_refs = refs[5:]
        h = _layer_norm(x_ref[...], g_ref[...], b_ref[...])
        h_ref[...] = h
    else:
        x_ref, w_ref = refs[:2]
        out_refs = refs[2:]
        h = x_ref[...]
    u = _dot(h.astype(BF16), w_ref[...])
    o = 0
    for ref, w in zip(out_refs, IN_WIDTHS):
        ref[...] = u[:, o:o + w]
        o += w
    out_refs[-1][...] = u[:, MLA_IN_W:MLA_IN_W + MISC_W].T[0:2 * SSD_HEADS, :]


def _in_proj(x, w, ln=None):
    n = x.shape[0]
    row = lambda wd: pl.BlockSpec((ROW_TILE, wd), lambda i: (i, 0))
    outs = [jax.ShapeDtypeStruct((n, wd), F32) for wd in IN_WIDTHS]
    outs.append(jax.ShapeDtypeStruct((2 * SSD_HEADS, n), F32))
    out_specs = [row(wd) for wd in IN_WIDTHS] + [pl.BlockSpec((2 * SSD_HEADS, ROW_TILE), lambda i: (0, i))]
    if ln is not None:
        args = (x, ln[0], ln[1], w)
        in_specs = [row(D_MODEL), _resident((1, D_MODEL)), _resident((1, D_MODEL)), _resident(w.shape)]
        outs = [jax.ShapeDtypeStruct((n, D_MODEL), F32)] + outs
        out_specs = [row(D_MODEL)] + out_specs
    else:
        args = (x, w)
        in_specs = [row(D_MODEL), _resident(w.shape)]
    return pl.pallas_call(
        functools.partial(_in_proj_kernel, ln is not None),
        out_shape=outs, grid=(n // ROW_TILE,), in_specs=in_specs, out_specs=out_specs,
        compiler_params=_cparams(("parallel",)), name="in_proj")(*args)


MLA_TQ = 512


def _mla_kernel(x_ref, misc_ref, ct_ref, st_ref, gq_ref, gkv_ref, wq_ref, wkv_ref, o_ref,
                q_scr, k_scr, v_scr, e_scr, l_scr):
    seq = x_ref.shape[0]
    scale = (MLA_NOPE + MLA_ROPE) ** -0.5 * math.log2(math.e)
    half = MLA_ROPE // 2
    ct = ct_ref[...]
    st = st_ref[...]
    cqn = _rms(x_ref[:, 0:256], gq_ref[...]).astype(BF16)
    ckvn = _rms(x_ref[:, 256:384], gkv_ref[...]).astype(BF16)
    lane = lax.broadcasted_iota(jnp.int32, (1, LANE), 1)
    on_rope = (lane >= KR_LANE) & (lane < KR_LANE + MLA_ROPE)

    def rotate(x):
        return x * ct + pltpu.roll(x, LANE - half, 1) * st

    k_rope = jnp.where(on_rope, rotate(misc_ref[...]), 0.0)
    for h in range(MLA_HEADS):
        sl = slice(h * LANE, (h + 1) * LANE)
        q_scr[:, sl] = (rotate(_dot(cqn, wq_ref[:, sl])) * scale).astype(BF16)
        k_scr[:, sl] = (_dot(ckvn, wkv_ref[:, sl]) + k_rope).astype(BF16)
        vsl = slice(MLA_HEADS * LANE + h * LANE, MLA_HEADS * LANE + (h + 1) * LANE)
        v_scr[:, sl] = _dot(ckvn, wkv_ref[:, vsl]).astype(BF16)

    def tile_rows(t):
        start = t * MLA_TQ if isinstance(t, int) else pl.multiple_of(t * MLA_TQ, MLA_TQ)
        return pl.ds(start, MLA_TQ)

    def scores(t, h):
        sl = slice(h * LANE, (h + 1) * LANE)
        s = _dot_nt(q_scr[tile_rows(t), sl], k_scr[:, sl])
        e = jnp.exp2(s - jnp.max(s, -1, keepdims=True))
        l_scr[h % 2] = jnp.sum(e, -1, keepdims=True)
        e_scr[h % 2] = e.astype(BF16)

    def values(t, h):
        sl = slice(h * LANE, (h + 1) * LANE)
        o = _dot(e_scr[h % 2], v_scr[:, sl]) / l_scr[h % 2]
        out = (tile_rows(t), slice((h // 2) * LANE, (h // 2 + 1) * LANE))
        if h % 2 == 0:
            o_ref[out] = o
        else:
            o_ref[out] += o

    def q_tile(t, last):
        for h in range(MLA_HEADS):
            values(t, h)
            if h + 1 < MLA_HEADS:
                scores(t, h + 1)
            elif not last:
                scores(t + 1, 0)

    def body(t, carry):
        q_tile(t, False)
        return carry

    n_tiles = seq // MLA_TQ
    scores(0, 0)
    lax.fori_loop(0, n_tiles - 1, body, 0)
    q_tile(n_tiles - 1, True)


def _mla(mla_in, misc, ct, st, gq, gkv, wq, wkv):
    bsz, seq, _ = mla_in.shape
    return pl.pallas_call(
        _mla_kernel,
        out_shape=jax.ShapeDtypeStruct((bsz, seq, GROUP_WIDTH), F32),
        grid=(bsz,),
        in_specs=[pl.BlockSpec((None, seq, MLA_IN_W), lambda b: (b, 0, 0)),
                  pl.BlockSpec((None, seq, MISC_W), lambda b: (b, 0, 0)),
                  _resident(ct.shape), _resident(st.shape), _resident(gq.shape), _resident(gkv.shape),
                  _resident(wq.shape), _resident(wkv.shape)],
        out_specs=pl.BlockSpec((None, seq, GROUP_WIDTH), lambda b: (b, 0, 0)),
        scratch_shapes=[pltpu.VMEM((seq, MLA_HEADS * LANE), BF16)] * 3
                       + [pltpu.VMEM((2, MLA_TQ, seq), BF16), pltpu.VMEM((2, MLA_TQ, 1), F32)],
        compiler_params=_cparams(("parallel",)), name="mla")(mla_in, misc, ct, st, gq, gkv, wq, wkv)


def _hy_filter_kernel(feat_ref, w1_ref, b1_ref, fr_ref, w2_ref, b2_ref, w3f_ref, w3b_ref, decay_ref, o_ref):
    seq = feat_ref.shape[0] // 2
    fr = fr_ref[...]
    h = jnp.sin(fr * (_dot_hi(feat_ref[...], w1_ref[...]) + b1_ref[...]))
    h = jnp.sin(fr * (_dot_hi(h, w2_ref[...]) + b2_ref[...]))
    for o in range(HY_ORDER):
        rows = slice(o * HY_WIDTH, (o + 1) * HY_WIDTH)
        o_ref[o, :, 0:seq] = _dot_nt_hi(w3b_ref[rows, :], h[0:seq]) * decay_ref[:, 0:seq]
        o_ref[o, :, seq:2 * seq] = _dot_nt_hi(w3f_ref[rows, :], h[seq:2 * seq]) * decay_ref[:, seq:2 * seq]


def _hy_filter(feat, w1, b1, fr, w2, b2, w3f, w3b, decay):
    return pl.pallas_call(
        _hy_filter_kernel,
        out_shape=jax.ShapeDtypeStruct((HY_ORDER, HY_WIDTH, feat.shape[0]), F32),
        compiler_params=pltpu.CompilerParams(vmem_limit_bytes=VMEM_LIMIT),
        name="hy_filter")(feat, w1, b1, fr, w2, b2, w3f, w3b, decay)


def _shift_rows(x, row, seq):
    prev = jnp.where(row == 0, 0.0, pltpu.roll(x, 1, 0))
    nxt = jnp.where(row == seq - 1, 0.0, pltpu.roll(x, seq - 1, 0))
    return prev, nxt


def _dwconv3(x, w, b, row, seq):
    prev, nxt = _shift_rows(x, row, seq)
    return prev * w[0:1] + x * w[1:2] + nxt * w[2:3] + b


def _hy_pre_kernel(u_ref, w_ref, b_ref, o_ref):
    seq = u_ref.shape[0]
    row = lax.broadcasted_iota(jnp.int32, (seq, HY_WIDTH), 0)
    for k in range(HY_ORDER + 1):
        sl = slice(k * HY_WIDTH, (k + 1) * HY_WIDTH)
        o_ref[sl, :] = _dwconv3(u_ref[:, sl], w_ref[:, sl], b_ref[:, sl], row, seq).T.astype(BF16)


def _hy_pre(hy_u, w, b):
    bsz, seq, wd = hy_u.shape
    return pl.pallas_call(
        _hy_pre_kernel,
        out_shape=jax.ShapeDtypeStruct((wd, bsz * seq), BF16),
        grid=(bsz,),
        in_specs=[pl.BlockSpec((None, seq, wd), lambda b: (b, 0, 0)), _resident(w.shape), _resident(b.shape)],
        out_specs=pl.BlockSpec((wd, seq), lambda b: (0, b)),
        compiler_params=_cparams(("parallel",)), name="hy_pre")(hy_u, w, b)


HY_ROWS16 = 16
HY_CH = 4


def _hy_conv_kernel(bias_ref, v_ref, x1_ref, x2_ref, f_ref, _after_ref, o_ref, *r_scrs):
    _, bsz, seq = v_ref.shape
    nblk = seq // HY_BLK
    c0 = pl.program_id(0) * HY_CH

    def build_toeplitz(slot, taps):
        r_scr = r_scrs[slot]
        k16 = jnp.broadcast_to(taps, (HY_ROWS16, 2 * seq))
        k16 = pltpu.roll(k16, 0, 1, stride=1, stride_axis=0)
        for g in range(LANE // HY_ROWS16):
            rows = slice(g * HY_ROWS16, (g + 1) * HY_ROWS16)
            r_scr[rows, :] = (k16 if g == 0 else pltpu.roll(k16, g * HY_ROWS16, 1)).astype(BF16)
        r_scr[LANE:2 * LANE, LANE:] = r_scr[0:LANE, 0:2 * seq - LANE]

    def long_conv(z, slot):
        r_scr = r_scrs[slot]
        zr = jnp.concatenate([z[:, j * HY_BLK:(j + 1) * HY_BLK] for j in range(nblk)], axis=0).astype(BF16)
        acc = [None] * nblk
        for d in range(-(nblk - 1), nblk):
            lo_in, hi_in = max(0, -d), nblk - max(0, d)
            w_d = r_scr[:, seq + d * HY_BLK: seq + (d + 1) * HY_BLK]
            part = _dot(zr[lo_in * bsz:hi_in * bsz, :], w_d)
            for j in range(lo_in, hi_in):
                piece = part[(j - lo_in) * bsz:(j - lo_in + 1) * bsz, :]
                acc[j + d] = piece if acc[j + d] is None else acc[j + d] + piece
        return jnp.concatenate(acc, axis=1)

    f_row = (pl.program_id(0) % (SUBLANE // HY_CH)) * HY_CH
    def build(ch):
        for order in range(HY_ORDER):
            build_toeplitz(ch * HY_ORDER + order, f_ref[order, pl.ds(f_row + ch, 1), :])

    build(0)
    for ch in range(HY_CH):
        if ch + 1 < HY_CH:
            build(ch + 1)
        z0 = v_ref[ch].astype(F32)
        z1 = x1_ref[ch].astype(F32) * (long_conv(z0, ch * HY_ORDER) + bias_ref[0, c0 + ch] * z0)
        o_ref[ch] = x2_ref[ch].astype(F32) * (long_conv(z1, ch * HY_ORDER + 1) + bias_ref[1, c0 + ch] * z1)


def _hy_conv(uct, filt, bias, after):
    _, bsz, seq = uct.shape
    per_slab = HY_WIDTH // HY_CH
    chan = lambda k: pl.BlockSpec((HY_CH, bsz, seq), lambda c, k=k: (c + k * per_slab, 0, 0))
    return pl.pallas_call(
        _hy_conv_kernel,
        out_shape=jax.ShapeDtypeStruct((HY_WIDTH, bsz, seq), F32),
        grid=(per_slab,),
        in_specs=[pl.BlockSpec(memory_space=pltpu.SMEM), chan(0), chan(1), chan(2),
                  pl.BlockSpec((HY_ORDER, SUBLANE, 2 * seq), lambda c: (0, c * HY_CH // SUBLANE, 0)),
                  pl.BlockSpec(memory_space=pl.ANY)],
        out_specs=pl.BlockSpec((HY_CH, bsz, seq), lambda c: (c, 0, 0)),
        scratch_shapes=[pltpu.VMEM((HY_BLK, 2 * seq), BF16)] * (HY_CH * HY_ORDER),
        compiler_params=_cparams(("parallel",)), name="hy_conv")(bias, uct, uct, uct, filt, after)


def _swa_kernel(sink_ref, x_ref, o_ref, k_scr, vlo_scr, vhi_scr, bias_scr):
    seq = x_ref.shape[0]
    qb = SWA_QB
    nb = seq // qb
    win = qb + 2 * BLOCK
    grp = SWA_HEADS // SWA_KV_HEADS
    log2e = math.log2(math.e)
    scale = SWA_HEAD_DIM ** -0.5 * log2e
    kvw = SWA_KV_HEADS * LANE
    zeros = jnp.zeros((BLOCK, kvw), BF16)
    lane = lax.broadcasted_iota(jnp.int32, (1, LANE), 1)
    lo_half = lane < SWA_HEAD_DIM
    for scr in (k_scr, vlo_scr, vhi_scr):
        scr[0:BLOCK, :] = zeros
        scr[BLOCK + seq:2 * BLOCK + seq, :] = zeros
    body = slice(BLOCK, BLOCK + seq)
    k_both = x_ref[:, 256:256 + LANE]
    k_flip = pltpu.roll(k_both, SWA_HEAD_DIM, 1)
    k_scr[body, 0:LANE] = jnp.where(lo_half, k_both, k_flip).astype(BF16)
    k_scr[body, LANE:2 * LANE] = jnp.where(lo_half, k_flip, k_both).astype(BF16)
    v_both = x_ref[:, 256 + LANE:256 + 2 * LANE]
    v_flip = pltpu.roll(v_both, SWA_HEAD_DIM, 1)
    vlo_scr[body, 0:LANE] = jnp.where(lo_half, v_both, 0.0).astype(BF16)
    vhi_scr[body, 0:LANE] = jnp.where(lo_half, 0.0, v_flip).astype(BF16)
    vlo_scr[body, LANE:2 * LANE] = jnp.where(lo_half, v_flip, 0.0).astype(BF16)
    vhi_scr[body, LANE:2 * LANE] = jnp.where(lo_half, 0.0, v_both).astype(BF16)

    qi = lax.broadcasted_iota(jnp.int32, (grp * qb, win), 0)
    ji = lax.broadcasted_iota(jnp.int32, (grp * qb, win), 1)
    upper = qi >= qb
    dist = jnp.abs(jnp.where(upper, qi - qb, qi) + BLOCK - ji)
    slopes = [(2.0 ** (-8.0 / SWA_HEADS)) ** (i + 1) * log2e for i in range(SWA_HEADS)]
    for j in range(SWA_KV_HEADS):
        slope = jnp.where(upper, slopes[grp * j + 1], slopes[grp * j])
        bias_scr[j] = jnp.where(dist <= SWA_WINDOW, -slope * dist.astype(F32), NEG_INF)
    col = lax.broadcasted_iota(jnp.int32, (1, win), 1)
    row_up = lax.broadcasted_iota(jnp.int32, (grp * qb, 1), 0) >= qb

    def block(n, edge):
        r0 = n * qb if isinstance(n, int) else pl.multiple_of(n * qb, qb)
        for j in range(SWA_KV_HEADS):
            sl = slice(j * LANE, (j + 1) * LANE)
            qp = x_ref[pl.ds(r0, qb), sl] * scale
            q2 = jnp.concatenate([jnp.where(lo_half, qp, 0.0), jnp.where(lo_half, 0.0, qp)], 0).astype(BF16)
            sc = _dot_nt(q2, k_scr[pl.ds(r0, win), sl]) + bias_scr[j]
            if edge == "first":
                sc = jnp.where(col >= BLOCK, sc, NEG_INF)
            elif edge == "last":
                sc = jnp.where(col < qb + BLOCK, sc, NEG_INF)
            sk = jnp.where(row_up, sink_ref[grp * j + 1], sink_ref[grp * j]) * log2e
            m = jnp.maximum(jnp.max(sc, -1, keepdims=True), sk)
            e = jnp.exp2(sc - m)
            den = jnp.sum(e, -1, keepdims=True) + jnp.exp2(sk - m)
            eb = e.astype(BF16)
            o_lo = _dot(eb[0:qb], vlo_scr[pl.ds(r0, win), sl]) / den[0:qb]
            o_hi = _dot(eb[qb:], vhi_scr[pl.ds(r0, win), sl]) / den[qb:]
            o_ref[pl.ds(r0, qb), sl] = o_lo + o_hi

    def inner(n, carry):
        block(n, None)
        return carry

    block(0, "first")
    lax.fori_loop(1, nb - 1, inner, 0, unroll=2)
    block(nb - 1, "last")


def _swa(swa_in, sink):
    bsz, seq, _ = swa_in.shape
    kv_scr = pltpu.VMEM((seq + 2 * BLOCK, SWA_KV_HEADS * LANE), BF16)
    grp = SWA_HEADS // SWA_KV_HEADS
    return pl.pallas_call(
        _swa_kernel,
        out_shape=jax.ShapeDtypeStruct((bsz, seq, GROUP_WIDTH), F32),
        grid=(bsz,),
        in_specs=[pl.BlockSpec(memory_space=pltpu.SMEM),
                  pl.BlockSpec((None, seq, SWA_IN_W), lambda b: (b, 0, 0))],
        out_specs=pl.BlockSpec((None, seq, GROUP_WIDTH), lambda b: (b, 0, 0)),
        scratch_shapes=[kv_scr, kv_scr, kv_scr,
                        pltpu.VMEM((SWA_KV_HEADS, grp * SWA_QB, SWA_QB + 2 * BLOCK), F32)],
        compiler_params=_cparams(("parallel",)), name="swa")(sink, swa_in)


def _softplus(x):
    return jnp.maximum(x, 0.0) + jnp.log1p(jnp.exp(-jnp.abs(x)))


def _split3(x):
    hi = x.astype(BF16)
    rest = x - hi.astype(F32)
    mid = rest.astype(BF16)
    return hi, mid, (rest - mid.astype(F32)).astype(BF16)


def _ssd_kernel(x_ref, dtr_ref, cw_ref, cb_ref, dtb_col_ref, alog_col_ref, dskip_ref, o_ref,
                act_scr, yf_scr, yb_scr, rows_scr, cols_scr):
    seq = x_ref.shape[0]
    q = SSD_CHUNK
    nch = seq // q
    nh = 2 * SSD_HEADS
    hpg = SSD_HEADS // SSD_GROUPS
    row = lax.broadcasted_iota(jnp.int32, (seq, GROUP_WIDTH), 0)
    for k in range(3):
        sl = slice(k * GROUP_WIDTH, (k + 1) * GROUP_WIDTH)
        xin = x_ref[:, 256 + k * GROUP_WIDTH:256 + (k + 1) * GROUP_WIDTH]
        act_scr[:, sl] = _silu(_dwconv3(xin, cw_ref[:, sl], cb_ref[:, sl], row, seq))

    ti = lax.broadcasted_iota(jnp.int32, (q, q), 0)
    si = lax.broadcasted_iota(jnp.int32, (q, q), 1)
    lower = si <= ti
    upper = si >= ti
    tri_l = jnp.where(lower, 1.0, 0.0).astype(BF16)
    tri_u = jnp.where(upper, 1.0, 0.0).astype(BF16)
    lane = lax.broadcasted_iota(jnp.int32, (1, LANE), 1)
    first = lane < SSD_HEAD_DIM

    dt_all = _softplus(dtr_ref[...] + dtb_col_ref[...])
    da_parts = _split3(dt_all * -jnp.exp(alog_col_ref[...]))
    row_tri = jnp.concatenate([tri_u, tri_l], axis=1)
    col_tri = jnp.concatenate([tri_l, tri_u], axis=0)
    rows_scr[...] = sum(_dot(part, row_tri) for part in da_parts)
    cols_scr[0:2 * q, :] = sum(_dot_nt(col_tri, part) for part in da_parts)
    cols_scr[2 * q:3 * q, :] = dt_all.T

    def per_head(col, h0):
        return jnp.where(first, col[:, h0:h0 + 1], col[:, h0 + 1:h0 + 2])

    def one_chunk(direction, c, carry, y_scr):
        hb = direction * SSD_HEADS
        keep = lower if direction == 0 else upper
        edge = q - 1 if direction == 0 else 0
        rows = slice(c * q, (c + 1) * q)
        heads = slice(c * nh, (c + 1) * nh)
        cs_col = cols_scr[direction * q:(direction + 1) * q, heads]
        cs_row = rows_scr[heads, direction * q:(direction + 1) * q]
        dt_col = cols_scr[2 * q:3 * q, heads]
        new = []
        for g in range(SSD_GROUPS):
            gs = slice(g * LANE, (g + 1) * LANE)
            h0 = hb + g * hpg
            xg = act_scr[rows, gs]
            bg = act_scr[rows, GROUP_WIDTH + g * LANE:GROUP_WIDTH + (g + 1) * LANE]
            cg = act_scr[rows, 2 * GROUP_WIDTH + g * LANE:2 * GROUP_WIDTH + (g + 1) * LANE]
            xdt = xg * per_head(dt_col, h0)
            cb = _dot_nt(cg.astype(BF16), bg.astype(BF16))
            y = None
            for r in range(hpg):
                seg = cs_col[:, h0 + r:h0 + r + 1] - cs_row[h0 + r:h0 + r + 1, :]
                lmat = jnp.exp(jnp.where(keep, seg, NEG_INF))
                xh = jnp.where(first if r == 0 else ~first, xdt, 0.0)
                part = _dot((cb * lmat).astype(BF16), xh.astype(BF16))
                y = part if y is None else y + part
            cs_g = per_head(cs_col, h0)
            total = cs_g[edge:edge + 1, :]
            y_scr[rows, gs] = y + _dot(cg.astype(BF16), carry[g].astype(BF16)) * jnp.exp(cs_g)
            st = _dot(bg.T.astype(BF16), (xdt * jnp.exp(total - cs_g)).astype(BF16))
            new.append(carry[g] * jnp.exp(total) + st)
        return new

    fwd = [jnp.zeros((SSD_STATE, LANE), F32) for _ in range(SSD_GROUPS)]
    bwd = [jnp.zeros((SSD_STATE, LANE), F32) for _ in range(SSD_GROUPS)]
    for i in range(nch):
        fwd = one_chunk(0, i, fwd, yf_scr)
        bwd = one_chunk(1, nch - 1 - i, bwd, yb_scr)

    dsum = dskip_ref[0:1, :] + dskip_ref[1:2, :]
    y = yf_scr[...] + yb_scr[...] + dsum * act_scr[:, 0:GROUP_WIDTH]
    o_ref[...] = y * _silu(x_ref[:, 0:GROUP_WIDTH])


def _ssd(ssd_in, dt_rows, cw, cb, dtb_col, alog_col, dskip):
    bsz, seq, _ = ssd_in.shape
    q = SSD_CHUNK
    nch = seq // q
    nh = 2 * SSD_HEADS
    small = (cw, cb, dtb_col, alog_col, dskip)
    return pl.pallas_call(
        _ssd_kernel,
        out_shape=jax.ShapeDtypeStruct((bsz, seq, GROUP_WIDTH), F32),
        grid=(bsz,),
        in_specs=[pl.BlockSpec((None, seq, SSD_IN_W), lambda b: (b, 0, 0)),
                  pl.BlockSpec((None, nch * nh, q), lambda b: (b, 0, 0))]
                 + [_resident(a.shape) for a in small],
        out_specs=pl.BlockSpec((None, seq, GROUP_WIDTH), lambda b: (b, 0, 0)),
        scratch_shapes=[pltpu.VMEM((seq, 3 * GROUP_WIDTH), F32)] + [pltpu.VMEM((seq, GROUP_WIDTH), F32)] * 2
                       + [pltpu.VMEM((nch * nh, 2 * q), F32), pltpu.VMEM((3 * q, nch * nh), F32)],
        compiler_params=_cparams(("parallel",)), name="ssd")(ssd_in, dt_rows, *small)


def _tail_kernel(tiles_per_seq, *refs):
    y_refs, refs = refs[:12], refs[12:]
    (hp_ref, h_ref, hn_ref, p_ref, mg_ref, wo_ref, l1g_ref, l1b_ref, wg_ref, wu_ref, cw_ref, cb_ref, wd_ref,
     lg_ref, lb_ref, pwp_ref, pwg_ref, pbg_ref, plg_ref, plb_ref, o_ref) = refs
    i = pl.program_id(0)
    ext = ROW_TILE + 2 * FFN_HALO
    main = slice(FFN_HALO, FFN_HALO + ROW_TILE)
    parts = []
    for k in range(N_MIX):
        prev_ref, main_ref, next_ref = y_refs[3 * k:3 * k + 3]
        if k == HYENA_SLOT:
            rows = [prev_ref[...].T[LANE - FFN_HALO:, :], main_ref[...].T, next_ref[...].T[:FFN_HALO, :]]
        else:
            rows = [prev_ref[...], main_ref[...], next_ref[...]]
        y = jnp.concatenate(rows, axis=0)
        parts.append(_rms(y, mg_ref[:, k * GROUP_WIDTH:(k + 1) * GROUP_WIDTH]).astype(BF16))
    h_in = jnp.concatenate([hp_ref[...], h_ref[...], hn_ref[...]], axis=0)
    h_ext = _layer_norm(ALPHA * h_in + _dot(jnp.concatenate(parts, axis=1), wo_ref[...]),
                        l1g_ref[...], l1b_ref[...])
    hext = h_ext.astype(BF16)
    h = h_ext[main]
    hb = hext[main]
    row = lax.broadcasted_iota(jnp.int32, (ROW_TILE, 1), 0)
    keep_prev = jnp.where((row == 0) & (i % tiles_per_seq == 0), 0.0, 1.0)
    keep_next = jnp.where((row == ROW_TILE - 1) & (i % tiles_per_seq == tiles_per_seq - 1), 0.0, 1.0)
    f = None
    for c0 in range(0, D_FF, FFN_CHUNK):
        sl = slice(c0, c0 + FFN_CHUNK)
        gate = _dot(hext, wg_ref[:, sl])
        g_prev = pltpu.roll(gate, 1, 0)[main]
        g_next = pltpu.roll(gate, ext - 1, 0)[main]
        g_cur = gate[main]
        conv = (g_prev * keep_prev * cw_ref[0:1, sl] + g_cur * cw_ref[1:2, sl]
                + g_next * keep_next * cw_ref[2:3, sl] + cb_ref[:, sl])
        act = (_silu(conv) * _dot(hb, wu_ref[:, sl])).astype(BF16)
        part = _dot(act, wd_ref[sl, :])
        f = part if f is None else f + part
    h = _layer_norm(ALPHA * h + f, lg_ref[...], lb_ref[...])
    gate = _dot(h.astype(BF16), pwg_ref[...]) + pbg_ref[...]
    e = _dot(p_ref[...].astype(BF16), pwp_ref[...]) / (1.0 + jnp.exp(-gate))
    o_ref[...] = _layer_norm(ALPHA * h + e, plg_ref[...], plb_ref[...])


def _halo_specs(width, n):
    per_tile = ROW_TILE // FFN_HALO
    last = n // FFN_HALO - 1
    return [pl.BlockSpec((FFN_HALO, width), lambda i: (jnp.maximum(i * per_tile - 1, 0), 0)),
            pl.BlockSpec((ROW_TILE, width), lambda i: (i, 0)),
            pl.BlockSpec((FFN_HALO, width), lambda i: (jnp.minimum((i + 1) * per_tile, last), 0))]


def _halo_specs_channel_major(width, n):
    per_tile = ROW_TILE // LANE
    last = n // LANE - 1
    return [pl.BlockSpec((width, LANE), lambda i: (0, jnp.maximum(i * per_tile - 1, 0))),
            pl.BlockSpec((width, ROW_TILE), lambda i: (0, i)),
            pl.BlockSpec((width, LANE), lambda i: (0, jnp.minimum((i + 1) * per_tile, last)))]


def _tail(ys, h, p, layer, seq, small):
    n = h.shape[0]
    y_specs, y_args = [], []
    for k, y in enumerate(ys):
        y_specs += _halo_specs_channel_major(GROUP_WIDTH, n) if k == HYENA_SLOT else _halo_specs(GROUP_WIDTH, n)
        y_args += [y, y, y]
    return pl.pallas_call(
        functools.partial(_tail_kernel, seq // ROW_TILE),
        out_shape=jax.ShapeDtypeStruct((n, D_MODEL), F32),
        grid=(n // ROW_TILE,),
        in_specs=y_specs + _halo_specs(D_MODEL, n)
                 + [pl.BlockSpec((None, ROW_TILE, PLE_DIM), lambda i: (layer, i, 0))]
                 + [_resident(a.shape) for a in small],
        out_specs=pl.BlockSpec((ROW_TILE, D_MODEL), lambda i: (i, 0)),
        compiler_params=_cparams(("parallel",)), name="tail")(*y_args, h, h, h, p, *small)


def _pad_cols(w, width):
    return jnp.pad(w, ((0, 0), (0, width - w.shape[1])))


def _in_proj_weight(w):
    o = np.cumsum([0, 256, 128, 32, 768, 256, 128, 128, 256, 768, 8])
    cq, ckv, kr, hy, sq, sk, sv, z, xbc, dt = (w[:, o[i]:o[i + 1]] for i in range(10))
    zc = lambda n: jnp.zeros((w.shape[0], n), w.dtype)
    half = MLA_ROPE // 2
    misc = [dt, zc(KR_LANE - dt.shape[1]), kr, kr[:, :half], zc(MISC_W - KR_LANE - MLA_ROPE - half)]
    return jnp.concatenate([cq, ckv] + misc + [hy, sq, sk, sv, z, xbc], 1).astype(BF16)


def _mla_weights(w_uq, w_ukv):
    r = w_uq.shape[0]
    half = MLA_ROPE // 2
    qh = w_uq.reshape(r, MLA_HEADS, MLA_NOPE + MLA_ROPE)
    zq = jnp.zeros((r, MLA_HEADS, LANE - MLA_NOPE - MLA_ROPE - half), w_uq.dtype)
    wq = jnp.concatenate([qh, qh[..., MLA_NOPE:MLA_NOPE + half], zq], -1).reshape(r, MLA_HEADS * LANE)
    rk = w_ukv.shape[0]
    kvh = w_ukv.reshape(rk, MLA_HEADS, MLA_NOPE + MLA_V)
    zk = jnp.zeros((rk, MLA_HEADS, LANE - MLA_NOPE), w_ukv.dtype)
    wk = jnp.concatenate([kvh[..., :MLA_NOPE], zk], -1).reshape(rk, MLA_HEADS * LANE)
    v = kvh[..., MLA_NOPE:]
    zv = jnp.zeros_like(v)
    odd = (jnp.arange(MLA_HEADS) % 2 == 1)[None, :, None]
    wv = jnp.concatenate([jnp.where(odd, zv, v), jnp.where(odd, v, zv)], -1).reshape(rk, MLA_HEADS * LANE)
    return wq.astype(BF16), jnp.concatenate([wk, wv], 1).astype(BF16)


def _rope_tables(seq):
    f32 = np.float32
    inv_freq = (f32(ROPE_THETA) ** (-np.arange(0, MLA_ROPE, 2, dtype=f32) / f32(MLA_ROPE))).astype(f32)
    ang = np.arange(seq, dtype=f32)[:, None] * inv_freq[None, :]
    cos, sin = np.cos(ang), np.sin(ang)
    one = np.ones((seq, MLA_NOPE), f32)
    zero = lambda n: np.zeros((seq, n), f32)
    ct = np.concatenate([one, cos, cos, zero(LANE - MLA_NOPE - MLA_ROPE)], 1)
    st = np.concatenate([zero(MLA_NOPE), -sin, sin, zero(LANE - MLA_NOPE - MLA_ROPE)], 1)
    return jnp.asarray(ct), jnp.asarray(st)


def _hy_tables(seq):
    f32 = np.float32
    t = np.linspace(0.0, 1.0, seq, dtype=f32)[:, None]
    ang = (f32(2.0 * math.pi) * np.arange(seq, dtype=f32)[:, None] / f32(seq)).astype(f32)
    bands = np.linspace(1e-4, HY_BANDS - 1, HY_BANDS, dtype=f32)[None, :]
    feat = np.concatenate([t, np.cos(bands * ang), -np.sin(bands * ang)], -1).astype(f32)
    max_decay = math.log(HY_TARGET) / HY_FAST_DECAY
    min_decay = math.log(HY_TARGET) / HY_SLOW_DECAY
    deltas = np.linspace(min_decay, max_decay, HY_WIDTH, dtype=f32)
    decay = np.exp(-t * np.abs(deltas)[None, :]).astype(f32)
    m = np.arange(2 * seq)
    pos = np.where(m < seq, np.minimum(seq - m, seq - 1), m - seq)
    decay_lag = np.where((m == 0)[:, None], f32(0.0), decay[pos])
    feat_lag = np.pad(feat[pos], ((0, 0), (0, LANE - feat.shape[1])))
    return jnp.asarray(feat_lag), jnp.asarray(np.ascontiguousarray(decay_lag.T))


def _pad2(w, rows, cols):
    return jnp.pad(w, ((0, rows - w.shape[0]), (0, cols - w.shape[1])))


def _hy_tap_weights(w3):
    w = w3.reshape(w3.shape[0], HY_ORDER, 2, HY_WIDTH)
    rows = lambda d: _pad_cols(w[:, :, d].reshape(w3.shape[0], HY_ORDER * HY_WIDTH).T, LANE)
    return rows(0), rows(1)


def kernel(x, p, emb_ln_g, emb_ln_b, w_in, mla_q_norm, mla_kv_norm, mla_w_uq, mla_w_ukv, hy_conv_w, hy_conv_b, hy_f_w1, hy_f_b1, hy_f_freq, hy_f_w2, hy_f_b2, hy_f_w3, hy_bias, swa_sink, ssd_conv_w, ssd_conv_b, ssd_dt_bias, ssd_a_log, ssd_d, mix_norm_g, w_out, ln1_g, ln1_b, ffn_w_gate, ffn_w_up, ffn_conv_w, ffn_conv_b, ffn_w_down, ln2_g, ln2_b, ple_w_proj, ple_w_gate, ple_b_gate, ln3_g, ln3_b):
    bsz, seq, d = x.shape
    n = bsz * seq
    row = lambda v: v.reshape(1, -1)
    ct, st = _rope_tables(seq)
    feat, decay = _hy_tables(seq)
    h = x.reshape(n, d)
    for i in range(DEPTH):
        w_in_i = _in_proj_weight(w_in[i])
        if i == 0:
            h, mla_in, misc, hy_u, swa_in, ssd_in, dt_t = _in_proj(h, w_in_i, (row(emb_ln_g), row(emb_ln_b)))
        else:
            mla_in, misc, hy_u, swa_in, ssd_in, dt_t = _in_proj(h, w_in_i)

        uct = _hy_pre(hy_u.reshape(bsz, seq, HY_IN_W), hy_conv_w[i], row(hy_conv_b[i]))
        w3f, w3b = _hy_tap_weights(hy_f_w3[i])
        filt = _hy_filter(feat, _pad2(hy_f_w1[i], LANE, LANE), _pad_cols(row(hy_f_b1[i]), LANE),
                          _pad_cols(row(hy_f_freq[i]), LANE), _pad2(hy_f_w2[i], LANE, LANE),
                          _pad_cols(row(hy_f_b2[i]), LANE), w3f, w3b, decay)

        wq, wkv = _mla_weights(mla_w_uq[i], mla_w_ukv[i])
        y_a = _mla(mla_in.reshape(bsz, seq, MLA_IN_W), misc.reshape(bsz, seq, MISC_W), ct, st,
                   row(mla_q_norm[i]), row(mla_kv_norm[i]), wq, wkv)

        y_c = _swa(swa_in.reshape(bsz, seq, SWA_IN_W), swa_sink[i])

        y_bt = _hy_conv(uct.reshape(HY_IN_W, bsz, seq), filt, hy_bias[i], y_c)

        nch = seq // SSD_CHUNK
        dt_rows = jnp.transpose(dt_t.reshape(2 * SSD_HEADS, bsz, nch, SSD_CHUNK), (1, 2, 0, 3))
        dt_rows = dt_rows.reshape(bsz, nch * 2 * SSD_HEADS, SSD_CHUNK)
        per_row = lambda v: jnp.tile(v.reshape(-1, 1), (nch, 1))
        dskip = jnp.repeat(ssd_d[i], SSD_HEAD_DIM, axis=1)
        y_d = _ssd(ssd_in.reshape(bsz, seq, SSD_IN_W), dt_rows, ssd_conv_w[i], row(ssd_conv_b[i]),
                   per_row(ssd_dt_bias[i]), per_row(ssd_a_log[i]), dskip)

        flat = lambda y: y.reshape(n, GROUP_WIDTH)
        tail_params = (row(mix_norm_g[i]), w_out[i].astype(BF16), row(ln1_g[i]), row(ln1_b[i]),
                       ffn_w_gate[i].astype(BF16), ffn_w_up[i].astype(BF16), ffn_conv_w[i], row(ffn_conv_b[i]),
                       ffn_w_down[i].astype(BF16), row(ln2_g[i]), row(ln2_b[i]),
                       ple_w_proj[i].astype(BF16), ple_w_gate[i].astype(BF16), row(ple_b_gate[i]),
                       row(ln3_g[i]), row(ln3_b[i]))
        h = _tail((flat(y_a), y_bt.reshape(HY_WIDTH, n), flat(y_c), flat(y_d)), h,
                  p.reshape(DEPTH, n, PLE_DIM), i, seq, tail_params)
    return h.reshape(bsz, seq, d)
```

```python
import functools
import math

import jax
import jax.numpy as jnp
import numpy as np
from jax import lax
from jax.experimental import pallas as pl
from jax.experimental.pallas import tpu as pltpu

F32 = jnp.float32
BF16 = jnp.bfloat16

D_MODEL = 1024
DEPTH = 2
PLE_DIM = 256
GROUP_WIDTH = 256
N_MIX = 4
HYENA_SLOT = 1
BLOCK = 128

MLA_HEADS = 4
MLA_Q_RANK = 256
MLA_KV_RANK = 128
MLA_NOPE = 64
MLA_ROPE = 32
MLA_V = 64
ROPE_THETA = 10000.0

HY_WIDTH = 256
HY_ORDER = 2
HY_EMB = 33
HY_BANDS = 16
HY_FAST_DECAY = 0.3
HY_SLOW_DECAY = 1.5
HY_TARGET = 1e-2
HY_BLK = 256

SWA_HEADS = 4
SWA_KV_HEADS = 2
SWA_HEAD_DIM = 64
SWA_WINDOW = 128
SWA_QB = 256

SSD_D_INNER = 256
SSD_HEAD_DIM = 64
SSD_HEADS = 4
SSD_GROUPS = 2
SSD_STATE = 128
SSD_CHUNK = 128

D_FF = 2816
LN_EPS = 1e-5
RMS_EPS = 1e-6
NEG_INF = -1e30
ALPHA = (2.0 * DEPTH) ** 0.25

LANE = 128
SUBLANE = 8
VMEM_LIMIT = 56 * 1024 * 1024

MLA_IN_W = 384
MISC_W = 128
HY_IN_W = 768
SWA_IN_W = 512
SSD_IN_W = 1024
IN_WIDTHS = (MLA_IN_W, MISC_W, HY_IN_W, SWA_IN_W, SSD_IN_W)
KR_LANE = MLA_NOPE

ROW_TILE = 512
FFN_HALO = 16
FFN_CHUNK = 1408


def _cparams(sem):
    return pltpu.CompilerParams(dimension_semantics=sem, vmem_limit_bytes=VMEM_LIMIT)


def _resident(shape):
    nd = len(shape)
    return pl.BlockSpec(shape, lambda *_: (0,) * nd, pipeline_mode=pl.Buffered(1))


def _layer_norm(x, g, b):
    mu = jnp.mean(x, -1, keepdims=True)
    xc = x - mu
    var = jnp.mean(xc * xc, -1, keepdims=True)
    return xc * lax.rsqrt(var + LN_EPS) * g + b


def _rms(x, g):
    return x * lax.rsqrt(jnp.mean(x * x, -1, keepdims=True) + RMS_EPS) * g


def _silu(x):
    return x / (1.0 + jnp.exp(-x))


def _dot(a, b):
    return jnp.dot(a, b, preferred_element_type=F32)


def _dot_nt(a, b):
    return lax.dot_general(a, b, (((1,), (1,)), ((), ())), preferred_element_type=F32)


def _dot_hi(a, b):
    return jnp.dot(a, b, preferred_element_type=F32, precision=lax.Precision.HIGHEST)


def _dot_nt_hi(a, b):
    return lax.dot_general(a, b, (((1,), (1,)), ((), ())), preferred_element_type=F32,
                           precision=lax.Precision.HIGHEST)


def _in_proj_kernel(apply_ln, *refs):
    if apply_ln:
        x_ref, g_ref, b_ref, w_ref, h_ref = refs[:5]
        out_refs = refs[5:]
        h = _layer_norm(x_ref[...], g_ref[...], b_ref[...])
        h_ref[...] = h
    else:
        x_ref, w_ref = refs[:2]
        out_refs = refs[2:]
        h = x_ref[...]
    u = _dot(h.astype(BF16), w_ref[...])
    o = 0
    for ref, w in zip(out_refs, IN_WIDTHS):
        ref[...] = u[:, o:o + w]
        o += w
    out_refs[-1][...] = u[:, MLA_IN_W:MLA_IN_W + MISC_W].T[0:2 * SSD_HEADS, :]


def _in_proj(x, w, ln=None):
    n = x.shape[0]
    row = lambda wd: pl.BlockSpec((ROW_TILE, wd), lambda i: (i, 0))
    outs = [jax.ShapeDtypeStruct((n, wd), F32) for wd in IN_WIDTHS]
    outs.append(jax.ShapeDtypeStruct((2 * SSD_HEADS, n), F32))
    out_specs = [row(wd) for wd in IN_WIDTHS] + [pl.BlockSpec((2 * SSD_HEADS, ROW_TILE), lambda i: (0, i))]
    if ln is not None:
        args = (x, ln[0], ln[1], w)
        in_specs = [row(D_MODEL), _resident((1, D_MODEL)), _resident((1, D_MODEL)), _resident(w.shape)]
        outs = [jax.ShapeDtypeStruct((n, D_MODEL), F32)] + outs
        out_specs = [row(D_MODEL)] + out_specs
    else:
        args = (x, w)
        in_specs = [row(D_MODEL), _resident(w.shape)]
    return pl.pallas_call(
        functools.partial(_in_proj_kernel, ln is not None),
        out_shape=outs, grid=(n // ROW_TILE,), in_specs=in_specs, out_specs=out_specs,
        compiler_params=_cparams(("parallel",)), name="in_proj")(*args)


MLA_TQ = 512


def _mla_kernel(x_ref, misc_ref, ct_ref, st_ref, gq_ref, gkv_ref, wq_ref, wkv_ref, o_ref,
                q_scr, k_scr, v_scr, e_scr, l_scr):
    seq = x_ref.shape[0]
    scale = (MLA_NOPE + MLA_ROPE) ** -0.5 * math.log2(math.e)
    half = MLA_ROPE // 2
    ct = ct_ref[...]
    st = st_ref[...]
    cqn = _rms(x_ref[:, 0:256], gq_ref[...]).astype(BF16)
    ckvn = _rms(x_ref[:, 256:384], gkv_ref[...]).astype(BF16)
    lane = lax.broadcasted_iota(jnp.int32, (1, LANE), 1)
    on_rope = (lane >= KR_LANE) & (lane < KR_LANE + MLA_ROPE)

    def rotate(x):
        return x * ct + pltpu.roll(x, LANE - half, 1) * st

    k_rope = jnp.where(on_rope, rotate(misc_ref[...]), 0.0)
    for h in range(MLA_HEADS):
        sl = slice(h * LANE, (h + 1) * LANE)
        q_scr[:, sl] = (rotate(_dot(cqn, wq_ref[:, sl])) * scale).astype(BF16)
        k_scr[:, sl] = (_dot(ckvn, wkv_ref[:, sl]) + k_rope).astype(BF16)
        vsl = slice(MLA_HEADS * LANE + h * LANE, MLA_HEADS * LANE + (h + 1) * LANE)
        v_scr[:, sl] = _dot(ckvn, wkv_ref[:, vsl]).astype(BF16)

    def tile_rows(t):
        start = t * MLA_TQ if isinstance(t, int) else pl.multiple_of(t * MLA_TQ, MLA_TQ)
        return pl.ds(start, MLA_TQ)

    def scores(t, h):
        sl = slice(h * LANE, (h + 1) * LANE)
        s = _dot_nt(q_scr[tile_rows(t), sl], k_scr[:, sl])
        e = jnp.exp2(s - jnp.max(s, -1, keepdims=True))
        l_scr[h % 2] = jnp.sum(e, -1, keepdims=True)
        e_scr[h % 2] = e.astype(BF16)

    def values(t, h):
        sl = slice(h * LANE, (h + 1) * LANE)
        o = _dot(e_scr[h % 2], v_scr[:, sl]) / l_scr[h % 2]
        out = (tile_rows(t), slice((h // 2) * LANE, (h // 2 + 1) * LANE))
        if h % 2 == 0:
            o_ref[out] = o
        else:
            o_ref[out] += o

    def q_tile(t, last):
        for h in range(MLA_HEADS):
            values(t, h)
            if h + 1 < MLA_HEADS:
                scores(t, h + 1)
            elif not last:
                scores(t + 1, 0)

    def body(t, carry):
        q_tile(t, False)
        return carry

    n_tiles = seq // MLA_TQ
    scores(0, 0)
    lax.fori_loop(0, n_tiles - 1, body, 0)
    q_tile(n_tiles - 1, True)


def _mla(mla_in, misc, ct, st, gq, gkv, wq, wkv):
    bsz, seq, _ = mla_in.shape
    return pl.pallas_call(
        _mla_kernel,
        out_shape=jax.ShapeDtypeStruct((bsz, seq, GROUP_WIDTH), F32),
        grid=(bsz,),
        in_specs=[pl.BlockSpec((None, seq, MLA_IN_W), lambda b: (b, 0, 0)),
                  pl.BlockSpec((None, seq, MISC_W), lambda b: (b, 0, 0)),
                  _resident(ct.shape), _resident(st.shape), _resident(gq.shape), _resident(gkv.shape),
                  _resident(wq.shape), _resident(wkv.shape)],
        out_specs=pl.BlockSpec((None, seq, GROUP_WIDTH), lambda b: (b, 0, 0)),
        scratch_shapes=[pltpu.VMEM((seq, MLA_HEADS * LANE), BF16)] * 3
                       + [pltpu.VMEM((2, MLA_TQ, seq), BF16), pltpu.VMEM((2, MLA_TQ, 1), F32)],
        compiler_params=_cparams(("parallel",)), name="mla")(mla_in, misc, ct, st, gq, gkv, wq, wkv)


def _hy_filter_kernel(feat_ref, w1_ref, b1_ref, fr_ref, w2_ref, b2_ref, w3f_ref, w3b_ref, decay_ref, o_ref):
    seq = feat_ref.shape[0] // 2
    fr = fr_ref[...]
    h = jnp.sin(fr * (_dot_hi(feat_ref[...], w1_ref[...]) + b1_ref[...]))
    h = jnp.sin(fr * (_dot_hi(h, w2_ref[...]) + b2_ref[...]))
    for o in range(HY_ORDER):
        rows = slice(o * HY_WIDTH, (o + 1) * HY_WIDTH)
        o_ref[o, :, 0:seq] = _dot_nt_hi(w3b_ref[rows, :], h[0:seq]) * decay_ref[:, 0:seq]
        o_ref[o, :, seq:2 * seq] = _dot_nt_hi(w3f_ref[rows, :], h[seq:2 * seq]) * decay_ref[:, seq:2 * seq]


def _hy_filter(feat, w1, b1, fr, w2, b2, w3f, w3b, decay):
    return pl.pallas_call(
        _hy_filter_kernel,
        out_shape=jax.ShapeDtypeStruct((HY_ORDER, HY_WIDTH, feat.shape[0]), F32),
        compiler_params=pltpu.CompilerParams(vmem_limit_bytes=VMEM_LIMIT),
        name="hy_filter")(feat, w1, b1, fr, w2, b2, w3f, w3b, decay)


def _shift_rows(x, row, seq):
    prev = jnp.where(row == 0, 0.0, pltpu.roll(x, 1, 0))
    nxt = jnp.where(row == seq - 1, 0.0, pltpu.roll(x, seq - 1, 0))
    return prev, nxt


def _dwconv3(x, w, b, row, seq):
    prev, nxt = _shift_rows(x, row, seq)
    return prev * w[0:1] + x * w[1:2] + nxt * w[2:3] + b


def _hy_pre_kernel(u_ref, w_ref, b_ref, o_ref):
    seq = u_ref.shape[0]
    row = lax.broadcasted_iota(jnp.int32, (seq, HY_WIDTH), 0)
    for k in range(HY_ORDER + 1):
        sl = slice(k * HY_WIDTH, (k + 1) * HY_WIDTH)
        o_ref[sl, :] = _dwconv3(u_ref[:, sl], w_ref[:, sl], b_ref[:, sl], row, seq).T


def _hy_pre(hy_u, w, b):
    bsz, seq, wd = hy_u.shape
    return pl.pallas_call(
        _hy_pre_kernel,
        out_shape=jax.ShapeDtypeStruct((wd, bsz * seq), F32),
        grid=(bsz,),
        in_specs=[pl.BlockSpec((None, seq, wd), lambda b: (b, 0, 0)), _resident(w.shape), _resident(b.shape)],
        out_specs=pl.BlockSpec((wd, seq), lambda b: (0, b)),
        compiler_params=_cparams(("parallel",)), name="hy_pre")(hy_u, w, b)


HY_ROWS16 = 16
HY_CH = 4


def _hy_conv_kernel(bias_ref, v_ref, x1_ref, x2_ref, f_ref, _after_ref, o_ref, r_scr):
    _, bsz, seq = v_ref.shape
    nblk = seq // HY_BLK
    c0 = pl.program_id(0) * HY_CH

    def build_toeplitz(slot, taps):
        k16 = jnp.broadcast_to(taps, (HY_ROWS16, 2 * seq))
        k16 = pltpu.roll(k16, 0, 1, stride=1, stride_axis=0)
        for g in range(LANE // HY_ROWS16):
            rows = slice(g * HY_ROWS16, (g + 1) * HY_ROWS16)
            r_scr[slot, rows, :] = (k16 if g == 0 else pltpu.roll(k16, g * HY_ROWS16, 1)).astype(BF16)
        r_scr[slot, LANE:2 * LANE, LANE:] = r_scr[slot, 0:LANE, 0:2 * seq - LANE]

    def long_conv(z, slot):
        zr = jnp.concatenate([z[:, j * HY_BLK:(j + 1) * HY_BLK] for j in range(nblk)], axis=0).astype(BF16)
        acc = [None] * nblk
        for d in range(-(nblk - 1), nblk):
            lo_in, hi_in = max(0, -d), nblk - max(0, d)
            w_d = r_scr[slot, :, seq + d * HY_BLK: seq + (d + 1) * HY_BLK]
            part = _dot(zr[lo_in * bsz:hi_in * bsz, :], w_d)
            for j in range(lo_in, hi_in):
                piece = part[(j - lo_in) * bsz:(j - lo_in + 1) * bsz, :]
                acc[j + d] = piece if acc[j + d] is None else acc[j + d] + piece
        return jnp.concatenate(acc, axis=1)

    f_row = (pl.program_id(0) % (SUBLANE // HY_CH)) * HY_CH
    for ch in range(HY_CH):
        for order in range(HY_ORDER):
            build_toeplitz(ch * HY_ORDER + order, f_ref[order, pl.ds(f_row + ch, 1), :])
    for ch in range(HY_CH):
        z0 = v_ref[ch]
        z1 = x1_ref[ch] * (long_conv(z0, ch * HY_ORDER) + bias_ref[0, c0 + ch] * z0)
        o_ref[ch] = x2_ref[ch] * (long_conv(z1, ch * HY_ORDER + 1) + bias_ref[1, c0 + ch] * z1)


def _hy_conv(uct, filt, bias, after):
    _, bsz, seq = uct.shape
    per_slab = HY_WIDTH // HY_CH
    chan = lambda k: pl.BlockSpec((HY_CH, bsz, seq), lambda c, k=k: (c + k * per_slab, 0, 0))
    return pl.pallas_call(
        _hy_conv_kernel,
        out_shape=jax.ShapeDtypeStruct((HY_WIDTH, bsz, seq), F32),
        grid=(per_slab,),
        in_specs=[pl.BlockSpec(memory_space=pltpu.SMEM), chan(0), chan(1), chan(2),
                  pl.BlockSpec((HY_ORDER, SUBLANE, 2 * seq), lambda c: (0, c * HY_CH // SUBLANE, 0)),
                  pl.BlockSpec(memory_space=pl.ANY)],
        out_specs=pl.BlockSpec((HY_CH, bsz, seq), lambda c: (c, 0, 0)),
        scratch_shapes=[pltpu.VMEM((HY_CH * HY_ORDER, HY_BLK, 2 * seq), BF16)],
        compiler_params=_cparams(("parallel",)), name="hy_conv")(bias, uct, uct, uct, filt, after)


def _swa_kernel(sink_ref, x_ref, o_ref, k_scr, vlo_scr, vhi_scr, bias_scr):
    seq = x_ref.shape[0]
    qb = SWA_QB
    nb = seq // qb
    win = qb + 2 * BLOCK
    grp = SWA_HEADS // SWA_KV_HEADS
    log2e = math.log2(math.e)
    scale = SWA_HEAD_DIM ** -0.5 * log2e
    kvw = SWA_KV_HEADS * LANE
    zeros = jnp.zeros((BLOCK, kvw), BF16)
    lane = lax.broadcasted_iota(jnp.int32, (1, LANE), 1)
    lo_half = lane < SWA_HEAD_DIM
    for scr in (k_scr, vlo_scr, vhi_scr):
        scr[0:BLOCK, :] = zeros
        scr[BLOCK + seq:2 * BLOCK + seq, :] = zeros
    body = slice(BLOCK, BLOCK + seq)
    k_both = x_ref[:, 256:256 + LANE]
    k_flip = pltpu.roll(k_both, SWA_HEAD_DIM, 1)
    k_scr[body, 0:LANE] = jnp.where(lo_half, k_both, k_flip).astype(BF16)
    k_scr[body, LANE:2 * LANE] = jnp.where(lo_half, k_flip, k_both).astype(BF16)
    v_both = x_ref[:, 256 + LANE:256 + 2 * LANE]
    v_flip = pltpu.roll(v_both, SWA_HEAD_DIM, 1)
    vlo_scr[body, 0:LANE] = jnp.where(lo_half, v_both, 0.0).astype(BF16)
    vhi_scr[body, 0:LANE] = jnp.where(lo_half, 0.0, v_flip).astype(BF16)
    vlo_scr[body, LANE:2 * LANE] = jnp.where(lo_half, v_flip, 0.0).astype(BF16)
    vhi_scr[body, LANE:2 * LANE] = jnp.where(lo_half, 0.0, v_both).astype(BF16)

    qi = lax.broadcasted_iota(jnp.int32, (grp * qb, win), 0)
    ji = lax.broadcasted_iota(jnp.int32, (grp * qb, win), 1)
    upper = qi >= qb
    dist = jnp.abs(jnp.where(upper, qi - qb, qi) + BLOCK - ji)
    slopes = [(2.0 ** (-8.0 / SWA_HEADS)) ** (i + 1) * log2e for i in range(SWA_HEADS)]
    for j in range(SWA_KV_HEADS):
        slope = jnp.where(upper, slopes[grp * j + 1], slopes[grp * j])
        bias_scr[j] = jnp.where(dist <= SWA_WINDOW, -slope * dist.astype(F32), NEG_INF)
    col = lax.broadcasted_iota(jnp.int32, (1, win), 1)
    row_up = lax.broadcasted_iota(jnp.int32, (grp * qb, 1), 0) >= qb

    def block(n, edge):
        r0 = n * qb if isinstance(n, int) else pl.multiple_of(n * qb, qb)
        for j in range(SWA_KV_HEADS):
            sl = slice(j * LANE, (j + 1) * LANE)
            qp = x_ref[pl.ds(r0, qb), sl] * scale
            q2 = jnp.concatenate([jnp.where(lo_half, qp, 0.0), jnp.where(lo_half, 0.0, qp)], 0).astype(BF16)
            sc = _dot_nt(q2, k_scr[pl.ds(r0, win), sl]) + bias_scr[j]
            if edge == "first":
                sc = jnp.where(col >= BLOCK, sc, NEG_INF)
            elif edge == "last":
                sc = jnp.where(col < qb + BLOCK, sc, NEG_INF)
            sk = jnp.where(row_up, sink_ref[grp * j + 1], sink_ref[grp * j]) * log2e
            m = jnp.maximum(jnp.max(sc, -1, keepdims=True), sk)
            e = jnp.exp2(sc - m)
            den = jnp.sum(e, -1, keepdims=True) + jnp.exp2(sk - m)
            eb = e.astype(BF16)
            o_lo = _dot(eb[0:qb], vlo_scr[pl.ds(r0, win), sl]) / den[0:qb]
            o_hi = _dot(eb[qb:], vhi_scr[pl.ds(r0, win), sl]) / den[qb:]
            o_ref[pl.ds(r0, qb), sl] = o_lo + o_hi

    block(0, "first")
    for n in range(1, nb - 1):
        block(n, None)
    block(nb - 1, "last")


def _swa(swa_in, sink):
    bsz, seq, _ = swa_in.shape
    kv_scr = pltpu.VMEM((seq + 2 * BLOCK, SWA_KV_HEADS * LANE), BF16)
    grp = SWA_HEADS // SWA_KV_HEADS
    return pl.pallas_call(
        _swa_kernel,
        out_shape=jax.ShapeDtypeStruct((bsz, seq, GROUP_WIDTH), F32),
        grid=(bsz,),
        in_specs=[pl.BlockSpec(memory_space=pltpu.SMEM),
                  pl.BlockSpec((None, seq, SWA_IN_W), lambda b: (b, 0, 0))],
        out_specs=pl.BlockSpec((None, seq, GROUP_WIDTH), lambda b: (b, 0, 0)),
        scratch_shapes=[kv_scr, kv_scr, kv_scr,
                        pltpu.VMEM((SWA_KV_HEADS, grp * SWA_QB, SWA_QB + 2 * BLOCK), F32)],
        compiler_params=_cparams(("parallel",)), name="swa")(sink, swa_in)


def _softplus(x):
    return jnp.maximum(x, 0.0) + jnp.log1p(jnp.exp(-jnp.abs(x)))


def _split3(x):
    hi = x.astype(BF16)
    rest = x - hi.astype(F32)
    mid = rest.astype(BF16)
    return hi, mid, (rest - mid.astype(F32)).astype(BF16)


def _ssd_kernel(x_ref, dtr_ref, cw_ref, cb_ref, dtb_col_ref, alog_col_ref, dskip_ref, o_ref,
                act_scr, yf_scr, yb_scr, rows_scr, cols_scr):
    seq = x_ref.shape[0]
    q = SSD_CHUNK
    nch = seq // q
    nh = 2 * SSD_HEADS
    hpg = SSD_HEADS // SSD_GROUPS
    row = lax.broadcasted_iota(jnp.int32, (seq, GROUP_WIDTH), 0)
    for k in range(3):
        sl = slice(k * GROUP_WIDTH, (k + 1) * GROUP_WIDTH)
        xin = x_ref[:, 256 + k * GROUP_WIDTH:256 + (k + 1) * GROUP_WIDTH]
        act_scr[:, sl] = _silu(_dwconv3(xin, cw_ref[:, sl], cb_ref[:, sl], row, seq))

    ti = lax.broadcasted_iota(jnp.int32, (q, q), 0)
    si = lax.broadcasted_iota(jnp.int32, (q, q), 1)
    lower = si <= ti
    upper = si >= ti
    tri_l = jnp.where(lower, 1.0, 0.0).astype(BF16)
    tri_u = jnp.where(upper, 1.0, 0.0).astype(BF16)
    lane = lax.broadcasted_iota(jnp.int32, (1, LANE), 1)
    first = lane < SSD_HEAD_DIM

    dt_all = _softplus(dtr_ref[...] + dtb_col_ref[...])
    da_parts = _split3(dt_all * -jnp.exp(alog_col_ref[...]))
    row_tri = jnp.concatenate([tri_u, tri_l], axis=1)
    col_tri = jnp.concatenate([tri_l, tri_u], axis=0)
    rows_scr[...] = sum(_dot(part, row_tri) for part in da_parts)
    cols_scr[0:2 * q, :] = sum(_dot_nt(col_tri, part) for part in da_parts)
    cols_scr[2 * q:3 * q, :] = dt_all.T

    def per_head(col, h0):
        return jnp.where(first, col[:, h0:h0 + 1], col[:, h0 + 1:h0 + 2])

    def one_chunk(direction, c, carry, y_scr):
        hb = direction * SSD_HEADS
        keep = lower if direction == 0 else upper
        edge = q - 1 if direction == 0 else 0
        rows = slice(c * q, (c + 1) * q)
        heads = slice(c * nh, (c + 1) * nh)
        cs_col = cols_scr[direction * q:(direction + 1) * q, heads]
        cs_row = rows_scr[heads, direction * q:(direction + 1) * q]
        dt_col = cols_scr[2 * q:3 * q, heads]
        new = []
        for g in range(SSD_GROUPS):
            gs = slice(g * LANE, (g + 1) * LANE)
            h0 = hb + g * hpg
            xg = act_scr[rows, gs]
            bg = act_scr[rows, GROUP_WIDTH + g * LANE:GROUP_WIDTH + (g + 1) * LANE]
            cg = act_scr[rows, 2 * GROUP_WIDTH + g * LANE:2 * GROUP_WIDTH + (g + 1) * LANE]
            xdt = xg * per_head(dt_col, h0)
            cb = _dot_nt(cg.astype(BF16), bg.astype(BF16))
            y = None
            for r in range(hpg):
                seg = cs_col[:, h0 + r:h0 + r + 1] - cs_row[h0 + r:h0 + r + 1, :]
                lmat = jnp.exp(jnp.where(keep, seg, NEG_INF))
                xh = jnp.where(first if r == 0 else ~first, xdt, 0.0)
                part = _dot((cb * lmat).astype(BF16), xh.astype(BF16))
                y = part if y is None else y + part
            cs_g = per_head(cs_col, h0)
            total = cs_g[edge:edge + 1, :]
            y_scr[rows, gs] = y + _dot(cg.astype(BF16), carry[g].astype(BF16)) * jnp.exp(cs_g)
            st = _dot(bg.T.astype(BF16), (xdt * jnp.exp(total - cs_g)).astype(BF16))
            new.append(carry[g] * jnp.exp(total) + st)
        return new

    fwd = [jnp.zeros((SSD_STATE, LANE), F32) for _ in range(SSD_GROUPS)]
    bwd = [jnp.zeros((SSD_STATE, LANE), F32) for _ in range(SSD_GROUPS)]
    for i in range(nch):
        fwd = one_chunk(0, i, fwd, yf_scr)
        bwd = one_chunk(1, nch - 1 - i, bwd, yb_scr)

    dsum = dskip_ref[0:1, :] + dskip_ref[1:2, :]
    y = yf_scr[...] + yb_scr[...] + dsum * act_scr[:, 0:GROUP_WIDTH]
    o_ref[...] = y * _silu(x_ref[:, 0:GROUP_WIDTH])


def _ssd(ssd_in, dt_rows, cw, cb, dtb_col, alog_col, dskip):
    bsz, seq, _ = ssd_in.shape
    q = SSD_CHUNK
    nch = seq // q
    nh = 2 * SSD_HEADS
    small = (cw, cb, dtb_col, alog_col, dskip)
    return pl.pallas_call(
        _ssd_kernel,
        out_shape=jax.ShapeDtypeStruct((bsz, seq, GROUP_WIDTH), F32),
        grid=(bsz,),
        in_specs=[pl.BlockSpec((None, seq, SSD_IN_W), lambda b: (b, 0, 0)),
                  pl.BlockSpec((None, nch * nh, q), lambda b: (b, 0, 0))]
                 + [_resident(a.shape) for a in small],
        out_specs=pl.BlockSpec((None, seq, GROUP_WIDTH), lambda b: (b, 0, 0)),
        scratch_shapes=[pltpu.VMEM((seq, 3 * GROUP_WIDTH), F32)] + [pltpu.VMEM((seq, GROUP_WIDTH), F32)] * 2
                       + [pltpu.VMEM((nch * nh, 2 * q), F32), pltpu.VMEM((3 * q, nch * nh), F32)],
        compiler_params=_cparams(("parallel",)), name="ssd")(ssd_in, dt_rows, *small)


def _tail_kernel(tiles_per_seq, *refs):
    y_refs, refs = refs[:12], refs[12:]
    (hp_ref, h_ref, hn_ref, p_ref, mg_ref, wo_ref, l1g_ref, l1b_ref, wg_ref, wu_ref, cw_ref, cb_ref, wd_ref,
     lg_ref, lb_ref, pwp_ref, pwg_ref, pbg_ref, plg_ref, plb_ref, o_ref) = refs
    i = pl.program_id(0)
    ext = ROW_TILE + 2 * FFN_HALO
    main = slice(FFN_HALO, FFN_HALO + ROW_TILE)
    parts = []
    for k in range(N_MIX):
        prev_ref, main_ref, next_ref = y_refs[3 * k:3 * k + 3]
        if k == HYENA_SLOT:
            rows = [prev_ref[...].T[LANE - FFN_HALO:, :], main_ref[...].T, next_ref[...].T[:FFN_HALO, :]]
        else:
            rows = [prev_ref[...], main_ref[...], next_ref[...]]
        y = jnp.concatenate(rows, axis=0)
        parts.append(_rms(y, mg_ref[:, k * GROUP_WIDTH:(k + 1) * GROUP_WIDTH]).astype(BF16))
    h_in = jnp.concatenate([hp_ref[...], h_ref[...], hn_ref[...]], axis=0)
    h_ext = _layer_norm(ALPHA * h_in + _dot(jnp.concatenate(parts, axis=1), wo_ref[...]),
                        l1g_ref[...], l1b_ref[...])
    hext = h_ext.astype(BF16)
    h = h_ext[main]
    hb = hext[main]
    row = lax.broadcasted_iota(jnp.int32, (ROW_TILE, 1), 0)
    keep_prev = jnp.where((row == 0) & (i % tiles_per_seq == 0), 0.0, 1.0)
    keep_next = jnp.where((row == ROW_TILE - 1) & (i % tiles_per_seq == tiles_per_seq - 1), 0.0, 1.0)
    f = None
    for c0 in range(0, D_FF, FFN_CHUNK):
        sl = slice(c0, c0 + FFN_CHUNK)
        gate = _dot(hext, wg_ref[:, sl])
        g_prev = pltpu.roll(gate, 1, 0)[main]
        g_next = pltpu.roll(gate, ext - 1, 0)[main]
        g_cur = gate[main]
        conv = (g_prev * keep_prev * cw_ref[0:1, sl] + g_cur * cw_ref[1:2, sl]
                + g_next * keep_next * cw_ref[2:3, sl] + cb_ref[:, sl])
        act = (_silu(conv) * _dot(hb, wu_ref[:, sl])).astype(BF16)
        part = _dot(act, wd_ref[sl, :])
        f = part if f is None else f + part
    h = _layer_norm(ALPHA * h + f, lg_ref[...], lb_ref[...])
    gate = _dot(h.astype(BF16), pwg_ref[...]) + pbg_ref[...]
    e = _dot(p_ref[...].astype(BF16), pwp_ref[...]) / (1.0 + jnp.exp(-gate))
    o_ref[...] = _layer_norm(ALPHA * h + e, plg_ref[...], plb_ref[...])


def _halo_specs(width, n):
    per_tile = ROW_TILE // FFN_HALO
    last = n // FFN_HALO - 1
    return [pl.BlockSpec((FFN_HALO, width), lambda i: (jnp.maximum(i * per_tile - 1, 0), 0)),
            pl.BlockSpec((ROW_TILE, width), lambda i: (i, 0)),
            pl.BlockSpec((FFN_HALO, width), lambda i: (jnp.minimum((i + 1) * per_tile, last), 0))]


def _halo_specs_channel_major(width, n):
    per_tile = ROW_TILE // LANE
    last = n // LANE - 1
    return [pl.BlockSpec((width, LANE), lambda i: (0, jnp.maximum(i * per_tile - 1, 0))),
            pl.BlockSpec((width, ROW_TILE), lambda i: (0, i)),
            pl.BlockSpec((width, LANE), lambda i: (0, jnp.minimum((i + 1) * per_tile, last)))]


def _tail(ys, h, p, layer, seq, small):
    n = h.shape[0]
    y_specs, y_args = [], []
    for k, y in enumerate(ys):
        y_specs += _halo_specs_channel_major(GROUP_WIDTH, n) if k == HYENA_SLOT else _halo_specs(GROUP_WIDTH, n)
        y_args += [y, y, y]
    return pl.pallas_call(
        functools.partial(_tail_kernel, seq // ROW_TILE),
        out_shape=jax.ShapeDtypeStruct((n, D_MODEL), F32),
        grid=(n // ROW_TILE,),
        in_specs=y_specs + _halo_specs(D_MODEL, n)
                 + [pl.BlockSpec((None, ROW_TILE, PLE_DIM), lambda i: (layer, i, 0))]
                 + [_resident(a.shape) for a in small],
        out_specs=pl.BlockSpec((ROW_TILE, D_MODEL), lambda i: (i, 0)),
        compiler_params=_cparams(("parallel",)), name="tail")(*y_args, h, h, h, p, *small)


def _pad_cols(w, width):
    return jnp.pad(w, ((0, 0), (0, width - w.shape[1])))


def _in_proj_weight(w):
    o = np.cumsum([0, 256, 128, 32, 768, 256, 128, 128, 256, 768, 8])
    cq, ckv, kr, hy, sq, sk, sv, z, xbc, dt = (w[:, o[i]:o[i + 1]] for i in range(10))
    zc = lambda n: jnp.zeros((w.shape[0], n), w.dtype)
    half = MLA_ROPE // 2
    misc = [dt, zc(KR_LANE - dt.shape[1]), kr, kr[:, :half], zc(MISC_W - KR_LANE - MLA_ROPE - half)]
    return jnp.concatenate([cq, ckv] + misc + [hy, sq, sk, sv, z, xbc], 1).astype(BF16)


def _mla_weights(w_uq, w_ukv):
    r = w_uq.shape[0]
    half = MLA_ROPE // 2
    qh = w_uq.reshape(r, MLA_HEADS, MLA_NOPE + MLA_ROPE)
    zq = jnp.zeros((r, MLA_HEADS, LANE - MLA_NOPE - MLA_ROPE - half), w_uq.dtype)
    wq = jnp.concatenate([qh, qh[..., MLA_NOPE:MLA_NOPE + half], zq], -1).reshape(r, MLA_HEADS * LANE)
    rk = w_ukv.shape[0]
    kvh = w_ukv.reshape(rk, MLA_HEADS, MLA_NOPE + MLA_V)
    zk = jnp.zeros((rk, MLA_HEADS, LANE - MLA_NOPE), w_ukv.dtype)
    wk = jnp.concatenate([kvh[..., :MLA_NOPE], zk], -1).reshape(rk, MLA_HEADS * LANE)
    v = kvh[..., MLA_NOPE:]
    zv = jnp.zeros_like(v)
    odd = (jnp.arange(MLA_HEADS) % 2 == 1)[None, :, None]
    wv = jnp.concatenate([jnp.where(odd, zv, v), jnp.where(odd, v, zv)], -1).reshape(rk, MLA_HEADS * LANE)
    return wq.astype(BF16), jnp.concatenate([wk, wv], 1).astype(BF16)


def _rope_tables(seq):
    f32 = np.float32
    inv_freq = (f32(ROPE_THETA) ** (-np.arange(0, MLA_ROPE, 2, dtype=f32) / f32(MLA_ROPE))).astype(f32)
    ang = np.arange(seq, dtype=f32)[:, None] * inv_freq[None, :]
    cos, sin = np.cos(ang), np.sin(ang)
    one = np.ones((seq, MLA_NOPE), f32)
    zero = lambda n: np.zeros((seq, n), f32)
    ct = np.concatenate([one, cos, cos, zero(LANE - MLA_NOPE - MLA_ROPE)], 1)
    st = np.concatenate([zero(MLA_NOPE), -sin, sin, zero(LANE - MLA_NOPE - MLA_ROPE)], 1)
    return jnp.asarray(ct), jnp.asarray(st)


def _hy_tables(seq):
    f32 = np.float32
    t = np.linspace(0.0, 1.0, seq, dtype=f32)[:, None]
    ang = (f32(2.0 * math.pi) * np.arange(seq, dtype=f32)[:, None] / f32(seq)).astype(f32)
    bands = np.linspace(1e-4, HY_BANDS - 1, HY_BANDS, dtype=f32)[None, :]
    feat = np.concatenate([t, np.cos(bands * ang), -np.sin(bands * ang)], -1).astype(f32)
    max_decay = math.log(HY_TARGET) / HY_FAST_DECAY
    min_decay = math.log(HY_TARGET) / HY_SLOW_DECAY
    deltas = np.linspace(min_decay, max_decay, HY_WIDTH, dtype=f32)
    decay = np.exp(-t * np.abs(deltas)[None, :]).astype(f32)
    m = np.arange(2 * seq)
    pos = np.where(m < seq, np.minimum(seq - m, seq - 1), m - seq)
    decay_lag = np.where((m == 0)[:, None], f32(0.0), decay[pos])
    feat_lag = np.pad(feat[pos], ((0, 0), (0, LANE - feat.shape[1])))
    return jnp.asarray(feat_lag), jnp.asarray(np.ascontiguousarray(decay_lag.T))


def _pad2(w, rows, cols):
    return jnp.pad(w, ((0, rows - w.shape[0]), (0, cols - w.shape[1])))


def _hy_tap_weights(w3):
    w = w3.reshape(w3.shape[0], HY_ORDER, 2, HY_WIDTH)
    rows = lambda d: _pad_cols(w[:, :, d].reshape(w3.shape[0], HY_ORDER * HY_WIDTH).T, LANE)
    return rows(0), rows(1)


def kernel(x, p, emb_ln_g, emb_ln_b, w_in, mla_q_norm, mla_kv_norm, mla_w_uq, mla_w_ukv, hy_conv_w, hy_conv_b, hy_f_w1, hy_f_b1, hy_f_freq, hy_f_w2, hy_f_b2, hy_f_w3, hy_bias, swa_sink, ssd_conv_w, ssd_conv_b, ssd_dt_bias, ssd_a_log, ssd_d, mix_norm_g, w_out, ln1_g, ln1_b, ffn_w_gate, ffn_w_up, ffn_conv_w, ffn_conv_b, ffn_w_down, ln2_g, ln2_b, ple_w_proj, ple_w_gate, ple_b_gate, ln3_g, ln3_b):
    bsz, seq, d = x.shape
    n = bsz * seq
    row = lambda v: v.reshape(1, -1)
    ct, st = _rope_tables(seq)
    feat, decay = _hy_tables(seq)
    h = x.reshape(n, d)
    for i in range(DEPTH):
        w_in_i = _in_proj_weight(w_in[i])
        if i == 0:
            h, mla_in, misc, hy_u, swa_in, ssd_in, dt_t = _in_proj(h, w_in_i, (row(emb_ln_g), row(emb_ln_b)))
        else:
            mla_in, misc, hy_u, swa_in, ssd_in, dt_t = _in_proj(h, w_in_i)

        uct = _hy_pre(hy_u.reshape(bsz, seq, HY_IN_W), hy_conv_w[i], row(hy_conv_b[i]))
        w3f, w3b = _hy_tap_weights(hy_f_w3[i])
        filt = _hy_filter(feat, _pad2(hy_f_w1[i], LANE, LANE), _pad_cols(row(hy_f_b1[i]), LANE),
                          _pad_cols(row(hy_f_freq[i]), LANE), _pad2(hy_f_w2[i], LANE, LANE),
                          _pad_cols(row(hy_f_b2[i]), LANE), w3f, w3b, decay)

        wq, wkv = _mla_weights(mla_w_uq[i], mla_w_ukv[i])
        y_a = _mla(mla_in.reshape(bsz, seq, MLA_IN_W), misc.reshape(bsz, seq, MISC_W), ct, st,
                   row(mla_q_norm[i]), row(mla_kv_norm[i]), wq, wkv)

        y_c = _swa(swa_in.reshape(bsz, seq, SWA_IN_W), swa_sink[i])

        y_bt = _hy_conv(uct.reshape(HY_IN_W, bsz, seq), filt, hy_bias[i], y_c)

        nch = seq // SSD_CHUNK
        dt_rows = jnp.transpose(dt_t.reshape(2 * SSD_HEADS, bsz, nch, SSD_CHUNK), (1, 2, 0, 3))
        dt_rows = dt_rows.reshape(bsz, nch * 2 * SSD_HEADS, SSD_CHUNK)
        per_row = lambda v: jnp.tile(v.reshape(-1, 1), (nch, 1))
        dskip = jnp.repeat(ssd_d[i], SSD_HEAD_DIM, axis=1)
        y_d = _ssd(ssd_in.reshape(bsz, seq, SSD_IN_W), dt_rows, ssd_conv_w[i], row(ssd_conv_b[i]),
                   per_row(ssd_dt_bias[i]), per_row(ssd_a_log[i]), dskip)

        flat = lambda y: y.reshape(n, GROUP_WIDTH)
        tail_params = (row(mix_norm_g[i]), w_out[i].astype(BF16), row(ln1_g[i]), row(ln1_b[i]),
                       ffn_w_gate[i].astype(BF16), ffn_w_up[i].astype(BF16), ffn_conv_w[i], row(ffn_conv_b[i]),
                       ffn_w_down[i].astype(BF16), row(ln2_g[i]), row(ln2_b[i]),
                       ple_w_proj[i].astype(BF16), ple_w_gate[i].astype(BF16), row(ple_b_gate[i]),
                       row(ln3_g[i]), row(ln3_b[i]))
        h = _tail((flat(y_a), y_bt.reshape(HY_WIDTH, n), flat(y_c), flat(y_d)), h,
                  p.reshape(DEPTH, n, PLE_DIM), i, seq, tail_params)
    return h.reshape(bsz, seq, d)
```
